```python
import jax, jax.numpy as jnp
from jax import lax
import numpy as np

D_MODEL = 2048
BATCH = 2
SEQ = 4096
DEPTH = 1

GRID_W = 64
CTX_LEN = 256
SSM_HEADS = 32
SSM_HEAD_DIM = 64
D_SSM = SSM_HEADS * SSM_HEAD_DIM
SSM_GROUPS = 8
D_STATE = 128
CONV_W = 5
CHUNK = 128
D_CONV_CH = D_SSM + 2 * SSM_GROUPS * D_STATE
ATTN_HEADS = 16
KV_HEADS = 4
HEAD_DIM = 128
D_ATTN = ATTN_HEADS * HEAD_DIM
D_KV = KV_HEADS * HEAD_DIM
ROPE_THETA = 10000.0
Q_BLOCK = 128
D_FF = -(-(8 * D_MODEL) // (3 * 256)) * 256
N_MOD = 6
EPS = 1e-6
PROJ_SIZES = (D_SSM, D_CONV_CH, 2 * SSM_HEADS, D_ATTN, D_KV, D_KV, 2 * D_MODEL)
D_IN = sum(PROJ_SIZES)

kernel_name = 'hybrid_ssd_gqa_prefix_dit_block'


def rms(t):
    tf = t.astype(jnp.float32)
    return (tf * lax.rsqrt(jnp.mean(tf * tf, axis=-1, keepdims=True) + EPS)).astype(t.dtype)


def modulate(h, shift, scale):
    return h * (1 + scale) + shift


def split_proj(p):
    idx = np.cumsum(PROJ_SIZES)[:-1].tolist()
    return jnp.split(p, idx, axis=-1)


def axial_rope_tables(rows, cols):
    n_freq = HEAD_DIM // 4
    inv = jnp.power(ROPE_THETA, -jnp.arange(n_freq, dtype=jnp.float32) / n_freq)
    ang = jnp.stack([rows.astype(jnp.float32)[:, None] * inv,
                     cols.astype(jnp.float32)[:, None] * inv], axis=1)
    return jnp.cos(ang), jnp.sin(ang)


def apply_rope(t, cos, sin):
    b, s, h, d = t.shape
    tr = t.reshape(b, s, h, 2, 2, d // 4)
    t1, t2 = tr[..., 0, :], tr[..., 1, :]
    cs = cos[None, :, None].astype(t.dtype)
    sn = sin[None, :, None].astype(t.dtype)
    return jnp.stack([t1 * cs - t2 * sn, t2 * cs + t1 * sn], axis=-2).reshape(b, s, h, d)


def dwconv(u, w, bias):
    pad = (CONV_W - 1) // 2
    out = lax.conv_general_dilated(u, w[:, None, :].astype(u.dtype), window_strides=(1,),
                                   padding=[(pad, pad)], dimension_numbers=('NWC', 'WIO', 'NWC'),
                                   feature_group_count=u.shape[-1])
    return out + bias.astype(u.dtype)


def ssd_prep(xbc, dt_raw, conv_w, conv_b, dt_bias):
    b, l, _ = xbc.shape
    xbc = jax.nn.silu(dwconv(xbc, conv_w, conv_b))
    gn = SSM_GROUPS * D_STATE
    xs = xbc[..., :D_SSM].reshape(b, l, SSM_HEADS, SSM_HEAD_DIM)
    bm = xbc[..., D_SSM:D_SSM + gn].reshape(b, l, SSM_GROUPS, D_STATE)
    cm = xbc[..., D_SSM + gn:].reshape(b, l, SSM_GROUPS, D_STATE)
    dt = jax.nn.softplus(dt_raw.reshape(b, l, 2, SSM_HEADS).astype(jnp.float32)
                         + dt_bias.astype(jnp.float32))
    return xs, bm, cm, dt


def ssd_scan(xs, dt, a_neg, bm, cm, h0):
    b, l, h, p = xs.shape
    g, n = bm.shape[2], bm.shape[3]
    nc = l // CHUNK
    f32 = jnp.float32
    rep = h // g
    bh = jnp.repeat(bm.astype(f32), rep, axis=2).reshape(b, nc, CHUNK, h, n)
    ch = jnp.repeat(cm.astype(f32), rep, axis=2).reshape(b, nc, CHUNK, h, n)
    xdt = (xs.astype(f32) * dt[..., None]).reshape(b, nc, CHUNK, h, p)
    a = (dt * a_neg).reshape(b, nc, CHUNK, h).transpose(0, 3, 1, 2)
    a_cum = jnp.cumsum(a, axis=-1)
    lower = jnp.tril(jnp.ones((CHUNK, CHUNK), dtype=bool))
    seg = jnp.exp(jnp.where(lower, a_cum[..., :, None] - a_cum[..., None, :], -jnp.inf))
    scores = jnp.einsum('bclhn,bcshn->bhcls', ch, bh) * seg
    y_diag = jnp.einsum('bhcls,bcshp->bclhp', scores, xdt)
    decay_to_end = jnp.exp(a_cum[..., -1:] - a_cum)
    chunk_states = jnp.einsum('bclhn,bhcl,bclhp->bchpn', bh, decay_to_end, xdt)
    chunk_decay = jnp.exp(a_cum[..., -1])

    def step(state, inp):
        dec, st = inp
        return dec[:, :, None, None] * state + st, state

    h_final, h_prev = lax.scan(step, h0.astype(f32),
                               (chunk_decay.transpose(2, 0, 1), chunk_states.transpose(1, 0, 2, 3, 4)))
    y_off = jnp.einsum('bclhn,cbhpn,bhcl->bclhp', ch, h_prev, jnp.exp(a_cum))
    return (y_diag + y_off).reshape(b, l, h, p), h_final


def bi_ssd(xs, bm, cm, dt, a_log, d_skip, h0_fwd, h0_bwd):
    a_neg = -jnp.exp(a_log.astype(jnp.float32))
    y_f, h_f = ssd_scan(xs, dt[:, :, 0], a_neg[0], bm, cm, h0_fwd)
    rev = lambda t: jnp.flip(t, axis=1)
    y_b, h_b = ssd_scan(rev(xs), rev(dt[:, :, 1]), a_neg[1], rev(bm), rev(cm), h0_bwd)
    y = y_f + rev(y_b) + d_skip.astype(jnp.float32)[:, None] * xs.astype(jnp.float32)
    return y.astype(xs.dtype), h_f, h_b


def ssm_output(y, z, norm_g):
    b, l = z.shape[:2]
    gsz = D_SSM // SSM_GROUPS
    gated = y.reshape(b, l, SSM_GROUPS, gsz) * jax.nn.silu(z).reshape(b, l, SSM_GROUPS, gsz)
    return rms(gated).reshape(b, l, D_SSM) * norm_g


def attend_blocks(q, keys, vals):
    b, s = q.shape[:2]
    nb = s // Q_BLOCK
    rep = ATTN_HEADS // KV_HEADS
    qb = q.reshape(b, nb, Q_BLOCK, KV_HEADS, rep, HEAD_DIM).transpose(1, 0, 2, 3, 4, 5)
    scale = HEAD_DIM ** -0.5

    def one_block(qi):
        sc = jnp.einsum('bqgrd,bkgd->bgrqk', qi, keys).astype(jnp.float32) * scale
        pr = jax.nn.softmax(sc, axis=-1).astype(vals.dtype)
        return jnp.einsum('bgrqk,bkgd->bqgrd', pr, vals)

    out = lax.map(one_block, qb)
    return out.transpose(1, 0, 2, 3, 4, 5).reshape(b, s, D_ATTN)


def branch_merge(gates, ssm_br, attn_br):
    g_ssm, g_attn = jnp.split(gates, 2, axis=-1)
    return jax.nn.sigmoid(g_ssm) * ssm_br + jax.nn.sigmoid(g_attn) * attn_br


def swiglu(h, w_in, w_out):
    gate, up = jnp.split(h @ w_in, 2, axis=-1)
    return (jax.nn.silu(gate) * up) @ w_out


def setup_inputs(seed: int = 0) -> dict:
    key = jax.random.key(seed)
    ks = jax.random.split(key, 24)
    f32 = jnp.float32

    def nrm(k, shape, fan_in, gain=1.0):
        return jax.random.normal(k, shape, f32) * (gain * fan_in ** -0.5)

    dt0 = jnp.exp(jax.random.uniform(ks[9], (DEPTH, 2, SSM_HEADS), f32,
                                     minval=float(np.log(1e-3)), maxval=float(np.log(1e-1))))
    dt_bias = dt0 + jnp.log(-jnp.expm1(-dt0))
    a_log = jnp.log(jax.random.uniform(ks[10], (DEPTH, 2, SSM_HEADS), f32, minval=1.0, maxval=16.0))
    return {
        'x': jax.random.normal(ks[0], (BATCH, SEQ, D_MODEL), f32),
        'c': jax.random.normal(ks[1], (BATCH, D_MODEL), f32),
        'ctx': jax.random.normal(ks[2], (BATCH, CTX_LEN, D_MODEL), f32),
        'c_ctx': jax.random.normal(ks[3], (D_MODEL,), f32),
        'w_mod': nrm(ks[4], (DEPTH, D_MODEL, N_MOD * D_MODEL), D_MODEL, 0.5),
        'b_mod': 0.02 * jax.random.normal(ks[5], (DEPTH, N_MOD * D_MODEL), f32),
        'norm1': 1.0 + 0.05 * jax.random.normal(ks[6], (DEPTH, D_MODEL), f32),
        'w_in': nrm(ks[7], (DEPTH, D_MODEL, D_IN), D_MODEL),
        'conv_w': nrm(ks[8], (DEPTH, CONV_W, D_CONV_CH), CONV_W),
        'conv_b': 0.02 * jax.random.normal(ks[11], (DEPTH, D_CONV_CH), f32),
        'dt_bias': dt_bias,
        'a_log': a_log,
        'd_skip': 1.0 + 0.1 * jax.random.normal(ks[12], (DEPTH, SSM_HEADS), f32),
        'ssm_norm': 1.0 + 0.05 * jax.random.normal(ks[13], (DEPTH, D_SSM), f32),
        'q_norm': 1.0 + 0.05 * jax.random.normal(ks[14], (DEPTH, HEAD_DIM), f32),
        'k_norm': 1.0 + 0.05 * jax.random.normal(ks[15], (DEPTH, HEAD_DIM), f32),
        'w_ssm_br': nrm(ks[16], (DEPTH, D_SSM, D_MODEL), D_SSM),
        'w_attn_br': nrm(ks[17], (DEPTH, D_ATTN, D_MODEL), D_ATTN),
        'w_o': nrm(ks[18], (DEPTH, D_MODEL, D_MODEL), D_MODEL),
        'norm2': 1.0 + 0.05 * jax.random.normal(ks[19], (DEPTH, D_MODEL), f32),
        'w_ffn_in': nrm(ks[20], (DEPTH, D_MODEL, 2 * D_FF), D_MODEL),
        'w_ffn_out': nrm(ks[21], (DEPTH, D_FF, D_MODEL), D_FF),
    }


def reference(x, c, ctx, c_ctx, w_mod, b_mod, norm1, w_in, conv_w, conv_b, dt_bias, a_log, d_skip,
              ssm_norm, q_norm, k_norm, w_ssm_br, w_attn_br, w_o, norm2, w_ffn_in, w_ffn_out):
    b, s, _ = x.shape
    n_ctx = ctx.shape[1]
    rows_n = s // GRID_W
    rows = jnp.repeat(jnp.arange(rows_n), GRID_W)
    cols = jnp.tile(jnp.arange(GRID_W), rows_n)
    cos, sin = axial_rope_tables(rows, cols)
    h0 = jnp.zeros((b, SSM_HEADS, SSM_HEAD_DIM, D_STATE), jnp.float32)

    for i in range(DEPTH):
        sh1, sc1, gt1, sh2, sc2, gt2 = [m[:, None, :] for m in
                                        jnp.split(jax.nn.silu(c) @ w_mod[i] + b_mod[i], N_MOD, axis=-1)]
        csh1, csc1, cgt1, csh2, csc2, cgt2 = [m[None, None, :] for m in
                                              jnp.split(jax.nn.silu(c_ctx) @ w_mod[i] + b_mod[i], N_MOD, axis=-1)]
        z_l, xbc_l, dtr_l, q_l, k_l, v_l, g_l = split_proj(modulate(rms(x) * norm1[i], sh1, sc1) @ w_in[i])
        z_c, xbc_c, dtr_c, q_c, k_c, v_c, g_c = split_proj(modulate(rms(ctx) * norm1[i], csh1, csc1) @ w_in[i])

        xs_c, bm_c, cm_c, dt_c = ssd_prep(xbc_c, dtr_c, conv_w[i], conv_b[i], dt_bias[i])
        xs_l, bm_l, cm_l, dt_l = ssd_prep(xbc_l, dtr_l, conv_w[i], conv_b[i], dt_bias[i])
        y_c, hf_c, hb_c = bi_ssd(xs_c, bm_c, cm_c, dt_c, a_log[i], d_skip[i], h0, h0)
        y_l, _, _ = bi_ssd(xs_l, bm_l, cm_l, dt_l, a_log[i], d_skip[i], hf_c, hb_c)
        ssm_br_l = ssm_output(y_l, z_l, ssm_norm[i]) @ w_ssm_br[i]

        ck = rms(k_c.reshape(b, n_ctx, KV_HEADS, HEAD_DIM)) * k_norm[i]
        cv = v_c.reshape(b, n_ctx, KV_HEADS, HEAD_DIM)
        ql = apply_rope(rms(q_l.reshape(b, s, ATTN_HEADS, HEAD_DIM)) * q_norm[i], cos, sin)
        kl = apply_rope(rms(k_l.reshape(b, s, KV_HEADS, HEAD_DIM)) * k_norm[i], cos, sin)
        vl = v_l.reshape(b, s, KV_HEADS, HEAD_DIM)
        keys = jnp.concatenate([ck, kl], axis=1)
        vals = jnp.concatenate([cv, vl], axis=1)
        attn_br_l = attend_blocks(ql, keys, vals) @ w_attn_br[i]

        x_mid = x + gt1 * (branch_merge(g_l, ssm_br_l, attn_br_l) @ w_o[i])
        x_new = x_mid + gt2 * swiglu(modulate(rms(x_mid) * norm2[i], sh2, sc2), w_ffn_in[i], w_ffn_out[i])

        if i < DEPTH - 1:
            ssm_br_c = ssm_output(y_c, z_c, ssm_norm[i]) @ w_ssm_br[i]
            qc = rms(q_c.reshape(b, n_ctx, ATTN_HEADS, HEAD_DIM)) * q_norm[i]
            attn_br_c = attend_blocks(qc, ck, cv) @ w_attn_br[i]
            ctx_mid = ctx + cgt1 * (branch_merge(g_c, ssm_br_c, attn_br_c) @ w_o[i])
            ctx = ctx_mid + cgt2 * swiglu(modulate(rms(ctx_mid) * norm2[i], csh2, csc2),
                                          w_ffn_in[i], w_ffn_out[i])
        x = x_new
    return x
```

```python
import functools

import jax
import jax.numpy as jnp
import numpy as np
from jax import lax
from jax.experimental import pallas as pl
from jax.experimental.pallas import tpu as pltpu

F32 = jnp.float32
BF16 = jnp.bfloat16

GRID_W = 64
SSM_HEADS = 32
SSM_HEAD_DIM = 64
SSM_GROUPS = 8
D_STATE = 128
CONV_W = 5
CHUNK = 128
ATTN_HEADS = 16
KV_HEADS = 4
HEAD_DIM = 128
ROPE_THETA = 10000.0
N_MOD = 6
EPS = 1e-6

HEADS_PER_GROUP = SSM_HEADS // SSM_GROUPS
GROUP_CH = HEADS_PER_GROUP * SSM_HEAD_DIM
Q_REP = ATTN_HEADS // KV_HEADS

VMEM_LIMIT_BYTES = 56 * 1024 * 1024
NEG_BIG = -1e30


def _cparams(n_grid):
    return pltpu.CompilerParams(dimension_semantics=("arbitrary",) * n_grid,
                                vmem_limit_bytes=VMEM_LIMIT_BYTES)


def _silu(v):
    return v * jax.nn.sigmoid(v)


def _mod_kernel(c_ref, w_ref, b_ref, o_ref):
    s = _silu(c_ref[...]).astype(BF16)
    o_ref[...] = jnp.dot(s, w_ref[...].astype(BF16), preferred_element_type=F32) + b_ref[...]


def _modulation(cvec, w_mod, b_mod):
    rows, d = cvec.shape
    n = w_mod.shape[1]
    tn = 1024
    return pl.pallas_call(
        _mod_kernel,
        grid=(n // tn,),
        in_specs=[pl.BlockSpec((rows, d), lambda j: (0, 0)),
                  pl.BlockSpec((d, tn), lambda j: (0, j)),
                  pl.BlockSpec((1, tn), lambda j: (0, j))],
        out_specs=pl.BlockSpec((rows, tn), lambda j: (0, j)),
        out_shape=jax.ShapeDtypeStruct((rows, n), F32),
        compiler_params=_cparams(1),
        name="modulation",
    )(cvec, w_mod, b_mod)


def _norm_mod(t, g, sh, sc):
    ms = jnp.mean(t * t, axis=-1, keepdims=True)
    return ((t * lax.rsqrt(ms + EPS)) * g) * (1.0 + sc) + sh


def _prenorm1_kernel(x_ref, ctx_ref, g_ref, sh_ref, sc_ref, o_ref, *, n_lat_tiles):
    i = pl.program_id(0)

    @pl.when(i < n_lat_tiles)
    def _():
        o_ref[...] = _norm_mod(x_ref[...], g_ref[...], sh_ref[0], sc_ref[0]).astype(o_ref.dtype)

    @pl.when(i >= n_lat_tiles)
    def _():
        o_ref[...] = _norm_mod(ctx_ref[...], g_ref[...], sh_ref[0], sc_ref[0]).astype(o_ref.dtype)


def _prenorm1(x2d, ctx2d, norm_g, mod3, batch, shift_blk, scale_blk):
    n_lat, d = x2d.shape
    n_ctx = ctx2d.shape[0]
    tr = 256
    lat_tiles, ctx_tiles = n_lat // tr, n_ctx // tr
    tiles_per_batch = lat_tiles // batch

    def mod_row(i):
        return jnp.where(i < lat_tiles, i // tiles_per_batch, batch)

    return pl.pallas_call(
        functools.partial(_prenorm1_kernel, n_lat_tiles=lat_tiles),
        grid=(lat_tiles + ctx_tiles,),
        in_specs=[pl.BlockSpec((tr, d), lambda i: (jnp.minimum(i, lat_tiles - 1), 0)),
                  pl.BlockSpec((tr, d), lambda i: (jnp.maximum(i - lat_tiles, 0), 0)),
                  pl.BlockSpec((1, d), lambda i: (0, 0)),
                  pl.BlockSpec((1, 1, d), lambda i: (mod_row(i), 0, shift_blk)),
                  pl.BlockSpec((1, 1, d), lambda i: (mod_row(i), 0, scale_blk))],
        out_specs=pl.BlockSpec((tr, d), lambda i: (i, 0)),
        out_shape=jax.ShapeDtypeStruct((n_lat + n_ctx, d), BF16),
        compiler_params=_cparams(1),
        name="prenorm1",
    )(x2d, ctx2d, norm_g, mod3, mod3)


def _prenorm2_kernel(x_ref, g_ref, sh_ref, sc_ref, o_ref):
    o_ref[...] = _norm_mod(x_ref[...], g_ref[...], sh_ref[0], sc_ref[0]).astype(o_ref.dtype)


def _prenorm2(x2d, norm_g, mod3, batch, shift_blk, scale_blk):
    n_lat, d = x2d.shape
    tr = 256
    tiles_per_batch = n_lat // tr // batch
    return pl.pallas_call(
        _prenorm2_kernel,
        grid=(n_lat // tr,),
        in_specs=[pl.BlockSpec((tr, d), lambda i: (i, 0)),
                  pl.BlockSpec((1, d), lambda i: (0, 0)),
                  pl.BlockSpec((1, 1, d), lambda i: (i // tiles_per_batch, 0, shift_blk)),
                  pl.BlockSpec((1, 1, d), lambda i: (i // tiles_per_batch, 0, scale_blk))],
        out_specs=pl.BlockSpec((tr, d), lambda i: (i, 0)),
        out_shape=jax.ShapeDtypeStruct((n_lat, d), BF16),
        compiler_params=_cparams(1),
        name="prenorm2",
    )(x2d, norm_g, mod3, mod3)


def _mm_kernel(*refs, n_a, which_a, n_extra, epilogue):
    n_w = len(which_a)
    a_refs = refs[:n_a]
    w_refs = refs[n_a:n_a + n_w]
    e_refs = refs[n_a + n_w:n_a + n_w + n_extra]
    o_ref = refs[n_a + n_w + n_extra]
    w_scr = refs[n_a + n_w + n_extra + 1:]

    @pl.when(pl.program_id(1) == 0)
    def _():
        for k in range(n_w):
            w_scr[k][...] = w_refs[k][...].astype(BF16)

    accs = [jnp.dot(a_refs[which_a[k]][...], w_scr[k][...], preferred_element_type=F32)
            for k in range(n_w)]
    o_ref[...] = epilogue(accs, e_refs).astype(o_ref.dtype)


def _matmul(a_list, w_list, extras, epilogue, n_out, out_dtype, *, m_rows, tm, tn, name):
    grid = (n_out // tn, m_rows // tm)
    in_specs, args = [], []
    for a, off in a_list:
        in_specs.append(pl.BlockSpec((tm, a.shape[1]), lambda j, m, off=off: (m + off, 0)))
        args.append(a)
    for w, off, _ in w_list:
        in_specs.append(pl.BlockSpec((w.shape[0], tn), lambda j, m, off=off: (0, j + off)))
        args.append(w)
    for e, blk, imap in extras:
        in_specs.append(pl.BlockSpec(blk, imap))
        args.append(e)
    kern = functools.partial(_mm_kernel, n_a=len(a_list), which_a=tuple(w[2] for w in w_list),
                             n_extra=len(extras), epilogue=epilogue)
    return pl.pallas_call(
        kern,
        grid=grid,
        in_specs=in_specs,
        out_specs=pl.BlockSpec((tm, tn), lambda j, m: (m, j)),
        out_shape=jax.ShapeDtypeStruct((m_rows, n_out), out_dtype),
        scratch_shapes=[pltpu.VMEM((w.shape[0], tn), BF16) for w, _, _ in w_list],
        compiler_params=_cparams(2),
        name=name,
    )(*args)


def _ep_plain(accs, e_refs):
    return accs[0]


def _ep_swiglu(accs, e_refs):
    return _silu(accs[0]) * accs[1]


def _ep_merge(accs, e_refs):
    g_ssm = e_refs[0][...].astype(F32)
    g_attn = e_refs[1][...].astype(F32)
    return jax.nn.sigmoid(g_ssm) * accs[0] + jax.nn.sigmoid(g_attn) * accs[1]


def _ep_gated_residual(accs, e_refs):
    return e_refs[0][...] + e_refs[1][0] * accs[0]


def _dt_kernel(w_ref, h_ref, b_ref, o_ref):
    raw = lax.dot_general(w_ref[...], h_ref[...], (((1,), (1,)), ((), ())),
                          preferred_element_type=F32) + b_ref[...]
    o_ref[...] = jnp.maximum(raw, 0.0) + jnp.log1p(jnp.exp(-jnp.abs(raw)))


def _dt_proj(w_dt_t, h_all, bias_col):
    n_rows, d = h_all.shape
    n_dt = w_dt_t.shape[0]
    tm = 512
    return pl.pallas_call(
        _dt_kernel,
        grid=(n_rows // tm,),
        in_specs=[pl.BlockSpec((n_dt, d), lambda m: (0, 0)),
                  pl.BlockSpec((tm, d), lambda m: (m, 0)),
                  pl.BlockSpec((n_dt, 1), lambda m: (0, 0))],
        out_specs=pl.BlockSpec((n_dt, tm), lambda m: (0, m)),
        out_shape=jax.ShapeDtypeStruct((n_dt, n_rows), F32),
        compiler_params=_cparams(1),
        name="dt_proj",
    )(w_dt_t, h_all, bias_col)


def _qk_prep_kernel(t_ref, g_ref, cos_ref, sin_ref, o_ref, *, n_heads, scale):
    cs = cos_ref[...]
    sn = sin_ref[...]
    lane = lax.broadcasted_iota(jnp.int32, cs.shape, 1)
    first_half = (lane % (HEAD_DIM // 2)) < (HEAD_DIM // 4)
    g = g_ref[...]
    for h in range(n_heads):
        t = t_ref[:, h * HEAD_DIM:(h + 1) * HEAD_DIM].astype(F32)
        tn = (t * lax.rsqrt(jnp.mean(t * t, axis=-1, keepdims=True) + EPS)) * g
        rot = jnp.where(first_half,
                        pltpu.roll(tn, HEAD_DIM - HEAD_DIM // 4, 1),
                        pltpu.roll(tn, HEAD_DIM // 4, 1))
        o_ref[:, h * HEAD_DIM:(h + 1) * HEAD_DIM] = ((tn * cs + rot * sn) * scale).astype(o_ref.dtype)


def _qk_prep(t, n_rows, n_heads, norm_g, cos_t, sin_t, table_block, scale, name):
    tm = 512
    width = n_heads * HEAD_DIM
    return pl.pallas_call(
        functools.partial(_qk_prep_kernel, n_heads=n_heads, scale=scale),
        grid=(n_rows // tm,),
        in_specs=[pl.BlockSpec((tm, width), lambda m: (m, 0)),
                  pl.BlockSpec((1, HEAD_DIM), lambda m: (0, 0)),
                  pl.BlockSpec((tm, HEAD_DIM), lambda m: (table_block(m), 0)),
                  pl.BlockSpec((tm, HEAD_DIM), lambda m: (table_block(m), 0))],
        out_specs=pl.BlockSpec((tm, width), lambda m: (m, 0)),
        out_shape=jax.ShapeDtypeStruct((n_rows, width), BF16),
        compiler_params=_cparams(1),
        name=name,
    )(t, norm_g, cos_t, sin_t)


def _attn_kernel(q_ref, kl_ref, kc_ref, vl_ref, vc_ref, o_ref, k_scr, v_scr, *, n_ctx):
    @pl.when(pl.program_id(2) == 0)
    def _():
        k_scr[0:n_ctx, :] = kc_ref[...]
        k_scr[n_ctx:, :] = kl_ref[...]
        v_scr[0:n_ctx, :] = vc_ref[...]
        v_scr[n_ctx:, :] = vl_ref[...]

    k = k_scr[...]
    v = v_scr[...]
    for r in range(Q_REP):
        q = q_ref[:, r * HEAD_DIM:(r + 1) * HEAD_DIM]
        sc = lax.dot_general(q, k, (((1,), (1,)), ((), ())), preferred_element_type=F32)
        mx = jnp.max(sc, axis=-1, keepdims=True)
        p = jnp.exp(sc - mx)
        den = jnp.sum(p, axis=-1, keepdims=True)
        o = jnp.dot(p.astype(BF16), v, preferred_element_type=F32)
        o_ref[:, r * HEAD_DIM:(r + 1) * HEAD_DIM] = (o / den).astype(o_ref.dtype)


def _attention(q, k_all, kv_all, batch, seq, n_ctx):
    tq = 256
    q_tiles = seq // tq
    ctx_blk0 = batch * seq // n_ctx
    gw = Q_REP * HEAD_DIM
    return pl.pallas_call(
        functools.partial(_attn_kernel, n_ctx=n_ctx),
        grid=(batch, KV_HEADS, q_tiles),
        in_specs=[pl.BlockSpec((tq, gw), lambda b, g, i: (b * q_tiles + i, g)),
                  pl.BlockSpec((seq, HEAD_DIM), lambda b, g, i: (b, g)),
                  pl.BlockSpec((n_ctx, HEAD_DIM), lambda b, g, i: (ctx_blk0 + b, g)),
                  pl.BlockSpec((seq, HEAD_DIM), lambda b, g, i: (b, KV_HEADS + g)),
                  pl.BlockSpec((n_ctx, HEAD_DIM), lambda b, g, i: (ctx_blk0 + b, KV_HEADS + g))],
        out_specs=pl.BlockSpec((tq, gw), lambda b, g, i: (b * q_tiles + i, g)),
        out_shape=jax.ShapeDtypeStruct((batch * seq, ATTN_HEADS * HEAD_DIM), BF16),
        scratch_shapes=[pltpu.VMEM((n_ctx + seq, HEAD_DIM), BF16),
                        pltpu.VMEM((n_ctx + seq, HEAD_DIM), BF16)],
        compiler_params=_cparams(3),
        name="attention",
    )(q, k_all, k_all, kv_all, kv_all)


def _split3_rows(v):
    hi = v.astype(BF16).astype(F32)
    r1 = v - hi
    mid = r1.astype(BF16).astype(F32)
    lo = (r1 - mid).astype(BF16).astype(F32)
    return hi, mid, lo


def _conv_silu(src_ref, n_rows, w_ref, b_ref, dst_ref, dst_off):
    n_blk = n_rows // CHUNK
    w = w_ref[...]
    bias = b_ref[...]
    pad = (CONV_W - 1) // 2

    def body(i, carry):
        r0 = pl.multiple_of(i * CHUNK, CHUNK)
        cur = src_ref[pl.ds(r0, CHUNK), :]
        prev = src_ref[pl.ds(pl.multiple_of(jnp.maximum(r0 - 8, 0), 8), 8), :]
        nxt = src_ref[pl.ds(pl.multiple_of(jnp.minimum(r0 + CHUNK, n_rows - 8), 8), 8), :]
        prev = jnp.where(i > 0, prev, 0.0)
        nxt = jnp.where(i < n_blk - 1, nxt, 0.0)
        cat = jnp.concatenate([prev, cur, nxt], axis=0)
        acc = jnp.broadcast_to(bias, cur.shape)
        for j in range(CONV_W):
            acc = acc + w[j:j + 1, :] * cat[8 - pad + j:8 - pad + j + CHUNK, :]
        dst_ref[pl.ds(pl.multiple_of(dst_off + r0, CHUNK), CHUNK), :] = _silu(acc).astype(dst_ref.dtype)
        return carry

    lax.fori_loop(0, n_blk, body, 0)


def _ssd_kernel(x_lat, b_lat, c_lat, x_ctx, b_ctx, c_ctx,
                cwx, cwb, cwc, cbx, cbb, cbc,
                dt_lat, dt_ctx, aneg_ref, dskip_ref, z_ref, ng_ref,
                o_ref,
                xc, bc, cc, y_scr, st_f, st_b, *, seq, n_ctx):
    nh = HEADS_PER_GROUP
    _conv_silu(x_ctx, n_ctx, cwx, cbx, xc, 0)
    _conv_silu(b_ctx, n_ctx, cwb, cbb, bc, 0)
    _conv_silu(c_ctx, n_ctx, cwc, cbc, cc, 0)
    _conv_silu(x_lat, seq, cwx, cbx, xc, n_ctx)
    _conv_silu(b_lat, seq, cwb, cbb, bc, n_ctx)
    _conv_silu(c_lat, seq, cwc, cbc, cc, n_ctx)

    row = lax.broadcasted_iota(jnp.int32, (CHUNK, CHUNK), 0)
    col = lax.broadcasted_iota(jnp.int32, (CHUNK, CHUNK), 1)
    triu = (row <= col).astype(BF16)
    lower = row >= col
    upper = row <= col
    sub8 = lax.broadcasted_iota(jnp.int32, (8, CHUNK), 0)
    is_fwd_row = sub8 < nh
    k_sel = lax.broadcasted_iota(jnp.int32, (32, 8 * CHUNK), 0)
    c_sel = lax.broadcasted_iota(jnp.int32, (32, 8 * CHUNK), 1)
    sel = ((k_sel < 24) & ((k_sel % 8) == (c_sel // CHUNK))).astype(BF16)
    k_exp = lax.broadcasted_iota(jnp.int32, (32, 2 * GROUP_CH), 0)
    c_exp = lax.broadcasted_iota(jnp.int32, (32, 2 * GROUP_CH), 1)
    expand = ((k_exp < 24) & ((k_exp % 8) == (c_exp // SSM_HEAD_DIM))).astype(BF16)
    zeros8 = jnp.zeros((8, CHUNK), F32)
    aneg = aneg_ref[...]
    dskip = dskip_ref[...]
    lane_head = lax.broadcasted_iota(jnp.int32, (CHUNK, GROUP_CH), 1) // SSM_HEAD_DIM

    def stack3(v):
        hi, mid, lo = _split3_rows(v)
        return jnp.concatenate([hi, mid, lo, zeros8], axis=0).astype(BF16)

    def t_dot(lhs_t, rhs):
        return lax.dot_general(lhs_t, rhs, (((0,), (0,)), ((), ())), preferred_element_type=F32)

    def chunk_scalars(dt8):
        a = dt8 * aneg
        res = jnp.dot(stack3(a), triu, preferred_element_type=F32)
        cum = res[0:8] + res[8:16] + res[16:24]
        total = jnp.broadcast_to(cum[:, CHUNK - 1:CHUNK], cum.shape)
        cumx = cum - a
        w8 = dt8 * jnp.exp(jnp.where(is_fwd_row, total - cum, cumx))
        e8 = jnp.exp(jnp.where(is_fwd_row, cum, total - cumx))
        return cum, cumx, w8, e8

    def expand_cols(v8):
        return t_dot(stack3(v8), expand)

    def fwd_chunk(r0, dt8, y_off_row):
        cum, cumx, w8, e8 = chunk_scalars(dt8)
        xs = xc[pl.ds(r0, CHUNK), :]
        bm = bc[pl.ds(r0, CHUNK), :]
        cm = cc[pl.ds(r0, CHUNK), :]
        w_col = expand_cols(w8)[:, :GROUP_CH]
        e_col = expand_cols(e8)[:, :GROUP_CH]
        st_prev = st_f[...]
        if y_off_row is not None:
            u8 = jnp.where(is_fwd_row, cum, cumx)
            cb = t_dot(stack3(u8), sel)
            g = lax.dot_general(cm, bm, (((1,), (1,)), ((), ())), preferred_element_type=F32)
            xs_b = xs.astype(BF16)
            w_parts, x_parts = [], []
            for r in range(nh):
                ef = cb[:, r * CHUNK:(r + 1) * CHUNK] - cum[r:r + 1, :]
                eb = cumx[nh + r:nh + r + 1, :] - cb[:, (nh + r) * CHUNK:(nh + r + 1) * CHUNK]
                df = jnp.exp(jnp.where(lower, ef, NEG_BIG)) * dt8[r:r + 1, :]
                db = jnp.exp(jnp.where(upper, eb, NEG_BIG)) * dt8[nh + r:nh + r + 1, :]
                w_parts.append((g * (df + db)).astype(BF16))
                x_parts.append(jnp.where(lane_head == r, xs_b, jnp.zeros_like(xs_b)))
            y = jnp.dot(jnp.concatenate(w_parts, axis=1), jnp.concatenate(x_parts, axis=0),
                        preferred_element_type=F32)
            y = y + e_col * jnp.dot(cm, st_prev.astype(BF16), preferred_element_type=F32)
            y_scr[pl.ds(y_off_row, CHUNK), :] = y + dskip * xs
        xw = (xs * w_col).astype(BF16)
        st_f[...] = st_prev * e_col[CHUNK - 1:CHUNK, :] + t_dot(bm, xw)

    def bwd_chunk(r0, dt8, y_off_row):
        _, _, w8, e8 = chunk_scalars(dt8)
        xs = xc[pl.ds(r0, CHUNK), :]
        bm = bc[pl.ds(r0, CHUNK), :]
        w_col = expand_cols(w8)[:, GROUP_CH:]
        e_col = expand_cols(e8)[:, GROUP_CH:]
        st_prev = st_b[...]
        if y_off_row is not None:
            cm = cc[pl.ds(r0, CHUNK), :]
            y = y_scr[pl.ds(y_off_row, CHUNK), :]
            y = y + e_col * jnp.dot(cm, st_prev.astype(BF16), preferred_element_type=F32)
            gated = y * _silu(z_ref[pl.ds(y_off_row, CHUNK), :].astype(F32))
            ms = jnp.mean(gated * gated, axis=-1, keepdims=True)
            o_ref[pl.ds(y_off_row, CHUNK), :] = ((gated * lax.rsqrt(ms + EPS)) * ng_ref[...]).astype(o_ref.dtype)
        xw = (xs * w_col).astype(BF16)
        st_b[...] = st_prev * e_col[0:1, :] + t_dot(bm, xw)

    n_ctx_chunks = n_ctx // CHUNK
    n_lat_chunks = seq // CHUNK
    st_f[...] = jnp.zeros_like(st_f)
    st_b[...] = jnp.zeros_like(st_b)

    for c in range(n_ctx_chunks):
        fwd_chunk(c * CHUNK, dt_ctx[:, c * CHUNK:(c + 1) * CHUNK], None)

    def fwd_body(c, carry):
        r = pl.multiple_of(c * CHUNK, CHUNK)
        fwd_chunk(pl.multiple_of(n_ctx + r, CHUNK), dt_lat[:, pl.ds(r, CHUNK)], r)
        return carry

    lax.fori_loop(0, n_lat_chunks, fwd_body, 0)

    for c in reversed(range(n_ctx_chunks)):
        bwd_chunk(c * CHUNK, dt_ctx[:, c * CHUNK:(c + 1) * CHUNK], None)

    def bwd_body(k, carry):
        r = pl.multiple_of((n_lat_chunks - 1 - k) * CHUNK, CHUNK)
        bwd_chunk(pl.multiple_of(n_ctx + r, CHUNK), dt_lat[:, pl.ds(r, CHUNK)], r)
        return carry

    lax.fori_loop(0, n_lat_chunks, bwd_body, 0)


def _ssd(xbc_all, conv_w, conv_b, dt_t, aneg_col, dskip_row, z, norm_g, batch, seq, n_ctx):
    d_ssm = SSM_HEADS * SSM_HEAD_DIM
    xblk = d_ssm // GROUP_CH
    b0 = d_ssm // D_STATE
    c0 = b0 + SSM_GROUPS
    ctx_blk0 = batch * seq // n_ctx
    del xblk
    in_specs = [
        pl.BlockSpec((seq, GROUP_CH), lambda b, g: (b, g)),
        pl.BlockSpec((seq, D_STATE), lambda b, g: (b, b0 + g)),
        pl.BlockSpec((seq, D_STATE), lambda b, g: (b, c0 + g)),
        pl.BlockSpec((n_ctx, GROUP_CH), lambda b, g: (ctx_blk0 + b, g)),
        pl.BlockSpec((n_ctx, D_STATE), lambda b, g: (ctx_blk0 + b, b0 + g)),
        pl.BlockSpec((n_ctx, D_STATE), lambda b, g: (ctx_blk0 + b, c0 + g)),
        pl.BlockSpec((CONV_W, GROUP_CH), lambda b, g: (0, g)),
        pl.BlockSpec((CONV_W, D_STATE), lambda b, g: (0, b0 + g)),
        pl.BlockSpec((CONV_W, D_STATE), lambda b, g: (0, c0 + g)),
        pl.BlockSpec((1, GROUP_CH), lambda b, g: (0, g)),
        pl.BlockSpec((1, D_STATE), lambda b, g: (0, b0 + g)),
        pl.BlockSpec((1, D_STATE), lambda b, g: (0, c0 + g)),
        pl.BlockSpec((8, seq), lambda b, g: (g, b)),
        pl.BlockSpec((8, n_ctx), lambda b, g: (g, ctx_blk0 + b)),
        pl.BlockSpec((8, 1), lambda b, g: (g, 0)),
        pl.BlockSpec((1, GROUP_CH), lambda b, g: (0, g)),
        pl.BlockSpec((seq, GROUP_CH), lambda b, g: (b, g)),
        pl.BlockSpec((1, GROUP_CH), lambda b, g: (0, g)),
    ]
    return pl.pallas_call(
        functools.partial(_ssd_kernel, seq=seq, n_ctx=n_ctx),
        grid=(batch, SSM_GROUPS),
        in_specs=in_specs,
        out_specs=pl.BlockSpec((seq, GROUP_CH), lambda b, g: (b, g)),
        out_shape=jax.ShapeDtypeStruct((batch * seq, d_ssm), BF16),
        scratch_shapes=[pltpu.VMEM((n_ctx + seq, GROUP_CH), F32),
                        pltpu.VMEM((n_ctx + seq, D_STATE), BF16),
                        pltpu.VMEM((n_ctx + seq, D_STATE), BF16),
                        pltpu.VMEM((seq, GROUP_CH), F32),
                        pltpu.VMEM((D_STATE, GROUP_CH), F32),
                        pltpu.VMEM((D_STATE, GROUP_CH), F32)],
        compiler_params=_cparams(2),
        name="ssd",
    )(xbc_all, xbc_all, xbc_all, xbc_all, xbc_all, xbc_all,
      conv_w, conv_w, conv_w, conv_b, conv_b, conv_b,
      dt_t, dt_t, aneg_col, dskip_row, z, norm_g)


def _rope_tables(seq):
    n_freq = HEAD_DIM // 4
    pos = jnp.arange(seq)
    rows = (pos // GRID_W).astype(F32)
    cols = (pos % GRID_W).astype(F32)
    inv = jnp.power(ROPE_THETA, -jnp.arange(n_freq, dtype=F32) / n_freq)
    ang_r = rows[:, None] * inv
    ang_c = cols[:, None] * inv
    cos_t = jnp.concatenate([jnp.cos(ang_r), jnp.cos(ang_r), jnp.cos(ang_c), jnp.cos(ang_c)], axis=1)
    sin_t = jnp.concatenate([-jnp.sin(ang_r), jnp.sin(ang_r), -jnp.sin(ang_c), jnp.sin(ang_c)], axis=1)
    return cos_t, sin_t


def kernel(x, c, ctx, c_ctx, w_mod, b_mod, norm1, w_in, conv_w, conv_b, dt_bias, a_log, d_skip,
           ssm_norm, q_norm, k_norm, w_ssm_br, w_attn_br, w_o, norm2, w_ffn_in, w_ffn_out):
    batch, seq, d = x.shape
    n_ctx = ctx.shape[1]
    n_lat = batch * seq
    n_ctx_rows = batch * n_ctx
    n_all = n_lat + n_ctx_rows
    d_ssm = SSM_HEADS * SSM_HEAD_DIM
    d_conv = d_ssm + 2 * SSM_GROUPS * D_STATE
    d_attn = ATTN_HEADS * HEAD_DIM
    d_kv = KV_HEADS * HEAD_DIM
    d_ff = w_ffn_out.shape[1]
    n_dt = 2 * SSM_HEADS
    o_xbc = d_ssm
    o_dt = o_xbc + d_conv
    o_q = o_dt + n_dt
    o_k = o_q + d_attn
    o_g = o_k + 2 * d_kv
    assert w_mod.shape[0] == 1, "single-layer block"

    cvec = jnp.concatenate([c, c_ctx[None, :], jnp.zeros((8 - batch - 1, d), F32)], axis=0)
    mod = _modulation(cvec, w_mod[0], b_mod[0][None, :])
    mod3 = mod[:batch + 1].reshape(batch + 1, 1, N_MOD * d)

    x2d = x.reshape(n_lat, d)
    h_all = _prenorm1(x2d, ctx.reshape(n_ctx_rows, d), norm1, mod3, batch, 0, 1)

    w_in0 = w_in[0]
    tn = 1024
    tm_lat = 1024 if seq % 1024 == 0 else 512
    tm_all = n_all // 8 if (n_all // 8) % 16 == 0 and n_all % 8 == 0 else 512
    z = _matmul([(h_all, 0)], [(w_in0, 0, 0)], [], _ep_plain, d_ssm, BF16,
                m_rows=n_lat, tm=tm_lat, tn=tn, name="proj_z")
    xbc_all = _matmul([(h_all, 0)], [(w_in0, o_xbc // tn, 0)], [], _ep_plain, d_conv, F32,
                      m_rows=n_all, tm=tm_all, tn=tn, name="proj_xbc")
    w_tail = w_in0[:, o_q:].astype(BF16)
    q_raw = _matmul([(h_all, 0)], [(w_tail, 0, 0)], [], _ep_plain, d_attn, BF16,
                    m_rows=n_lat, tm=tm_lat, tn=tn, name="proj_q")
    kv_all = _matmul([(h_all, 0)], [(w_tail, d_attn // tn, 0)], [], _ep_plain, 2 * d_kv, BF16,
                     m_rows=n_all, tm=tm_all, tn=tn, name="proj_kv")
    gates = _matmul([(h_all, 0)], [(w_tail, (d_attn + 2 * d_kv) // tn, 0)], [], _ep_plain, 2 * d, BF16,
                    m_rows=n_lat, tm=tm_lat, tn=tn, name="proj_gates")

    perm = np.array([dr * SSM_HEADS + g * HEADS_PER_GROUP + r
                     for g in range(SSM_GROUPS) for dr in range(2) for r in range(HEADS_PER_GROUP)])
    w_dt_t = w_in0[:, o_dt:o_dt + n_dt][:, perm].T.astype(BF16)
    dt_bias_col = dt_bias[0].reshape(n_dt)[perm][:, None]
    aneg_col = (-jnp.exp(a_log[0].astype(F32))).reshape(n_dt)[perm][:, None]
    dt_t = _dt_proj(w_dt_t, h_all, dt_bias_col)

    dskip_row = jnp.repeat(d_skip[0].astype(F32), SSM_HEAD_DIM)[None, :]
    y_norm = _ssd(xbc_all, conv_w[0], conv_b[0][None, :], dt_t, aneg_col, dskip_row, z,
                  ssm_norm, batch, seq, n_ctx)

    cos_t, sin_t = _rope_tables(seq)
    tq = 512
    seq_tiles = seq // tq
    q = _qk_prep(q_raw, n_lat, ATTN_HEADS, q_norm, cos_t, sin_t,
                 lambda m: m % seq_tiles, HEAD_DIM ** -0.5, "q_prep")
    cos_k = jnp.concatenate([cos_t, jnp.ones((tq, HEAD_DIM), F32)], axis=0)
    sin_k = jnp.concatenate([sin_t, jnp.zeros((tq, HEAD_DIM), F32)], axis=0)
    lat_tiles = n_lat // tq
    k_all = _qk_prep(kv_all, n_all, KV_HEADS, k_norm, cos_k, sin_k,
                     lambda m: jnp.where(m < lat_tiles, m % seq_tiles, seq_tiles), 1.0, "k_prep")
    attn = _attention(q, k_all, kv_all, batch, seq, n_ctx)

    tn_m = 512
    merged = _matmul([(y_norm, 0), (attn, 0)], [(w_ssm_br[0], 0, 0), (w_attn_br[0], 0, 1)],
                     [(gates, (tm_lat, tn_m), lambda j, m: (m, j)),
                      (gates, (tm_lat, tn_m), lambda j, m: (m, j + d // tn_m))],
                     _ep_merge, d, BF16, m_rows=n_lat, tm=tm_lat, tn=tn_m, name="branch_merge")
    tiles_per_batch = seq // tm_lat
    x_mid = _matmul([(merged, 0)], [(w_o[0], 0, 0)],
                    [(x2d, (tm_lat, tn), lambda j, m: (m, j)),
                     (mod3, (1, 1, tn), lambda j, m: (m // tiles_per_batch, 0, 2 * (d // tn) + j))],
                    _ep_gated_residual, d, F32, m_rows=n_lat, tm=tm_lat, tn=tn, name="out_proj")

    h2 = _prenorm2(x_mid, norm2, mod3, batch, 3, 4)
    tn_f = 512
    act = _matmul([(h2, 0)], [(w_ffn_in[0], 0, 0), (w_ffn_in[0], d_ff // tn_f, 0)], [], _ep_swiglu,
                  d_ff, BF16, m_rows=n_lat, tm=tm_lat, tn=tn_f, name="ffn_in")
    tm_o = 512
    tiles_per_batch_o = seq // tm_o
    out = _matmul([(act, 0)], [(w_ffn_out[0], 0, 0)],
                  [(x_mid, (tm_o, tn_f), lambda j, m: (m, j)),
                   (mod3, (1, 1, tn_f), lambda j, m: (m // tiles_per_batch_o, 0, 5 * (d // tn_f) + j))],
                  _ep_gated_residual, d, F32, m_rows=n_lat, tm=tm_o, tn=tn_f, name="ffn_out")
    return out.reshape(batch, seq, d)
```

```python
import functools

import jax
import jax.numpy as jnp
import numpy as np
from jax import lax
from jax.experimental import pallas as pl
from jax.experimental.pallas import tpu as pltpu

F32 = jnp.float32
BF16 = jnp.bfloat16

GRID_W = 64
SSM_HEADS = 32
SSM_HEAD_DIM = 64
SSM_GROUPS = 8
D_STATE = 128
CONV_W = 5
CHUNK = 128
ATTN_HEADS = 16
KV_HEADS = 4
HEAD_DIM = 128
ROPE_THETA = 10000.0
N_MOD = 6
EPS = 1e-6

HEADS_PER_GROUP = SSM_HEADS // SSM_GROUPS
GROUP_CH = HEADS_PER_GROUP * SSM_HEAD_DIM
Q_REP = ATTN_HEADS // KV_HEADS

VMEM_LIMIT_BYTES = 56 * 1024 * 1024
NEG_BIG = -1e30
LOG2_E = 1.4426950408889634
SSD_UNROLL = 8


def _cparams(n_grid):
    return pltpu.CompilerParams(dimension_semantics=("arbitrary",) * n_grid,
                                vmem_limit_bytes=VMEM_LIMIT_BYTES)


def _silu(v):
    return v * jax.nn.sigmoid(v)


def _mod_kernel(c_ref, w_ref, b_ref, o_ref):
    s = _silu(c_ref[...]).astype(BF16)
    o_ref[...] = jnp.dot(s, w_ref[...].astype(BF16), preferred_element_type=F32) + b_ref[...]


def _modulation(cvec, w_mod, b_mod):
    rows, d = cvec.shape
    n = w_mod.shape[1]
    tn = 1024
    return pl.pallas_call(
        _mod_kernel,
        grid=(n // tn,),
        in_specs=[pl.BlockSpec((rows, d), lambda j: (0, 0)),
                  pl.BlockSpec((d, tn), lambda j: (0, j)),
                  pl.BlockSpec((1, tn), lambda j: (0, j))],
        out_specs=pl.BlockSpec((rows, tn), lambda j: (0, j)),
        out_shape=jax.ShapeDtypeStruct((rows, n), F32),
        compiler_params=_cparams(1),
        name="modulation",
    )(cvec, w_mod, b_mod)


def _norm_mod(t, g, sh, sc):
    ms = jnp.mean(t * t, axis=-1, keepdims=True)
    return ((t * lax.rsqrt(ms + EPS)) * g) * (1.0 + sc) + sh


def _prenorm1_kernel(x_ref, ctx_ref, g_ref, sh_ref, sc_ref, o_ref, *, n_lat_tiles):
    i = pl.program_id(0)

    @pl.when(i < n_lat_tiles)
    def _():
        o_ref[...] = _norm_mod(x_ref[...], g_ref[...], sh_ref[0], sc_ref[0]).astype(o_ref.dtype)

    @pl.when(i >= n_lat_tiles)
    def _():
        o_ref[...] = _norm_mod(ctx_ref[...], g_ref[...], sh_ref[0], sc_ref[0]).astype(o_ref.dtype)


def _prenorm1(x2d, ctx2d, norm_g, mod3, batch, shift_blk, scale_blk):
    n_lat, d = x2d.shape
    n_ctx = ctx2d.shape[0]
    tr = 256
    lat_tiles, ctx_tiles = n_lat // tr, n_ctx // tr
    tiles_per_batch = lat_tiles // batch

    def mod_row(i):
        return jnp.where(i < lat_tiles, i // tiles_per_batch, batch)

    return pl.pallas_call(
        functools.partial(_prenorm1_kernel, n_lat_tiles=lat_tiles),
        grid=(lat_tiles + ctx_tiles,),
        in_specs=[pl.BlockSpec((tr, d), lambda i: (jnp.minimum(i, lat_tiles - 1), 0)),
                  pl.BlockSpec((tr, d), lambda i: (jnp.maximum(i - lat_tiles, 0), 0)),
                  pl.BlockSpec((1, d), lambda i: (0, 0)),
                  pl.BlockSpec((1, 1, d), lambda i: (mod_row(i), 0, shift_blk)),
                  pl.BlockSpec((1, 1, d), lambda i: (mod_row(i), 0, scale_blk))],
        out_specs=pl.BlockSpec((tr, d), lambda i: (i, 0)),
        out_shape=jax.ShapeDtypeStruct((n_lat + n_ctx, d), BF16),
        compiler_params=_cparams(1),
        name="prenorm1",
    )(x2d, ctx2d, norm_g, mod3, mod3)


def _prenorm2_kernel(x_ref, g_ref, sh_ref, sc_ref, o_ref):
    o_ref[...] = _norm_mod(x_ref[...], g_ref[...], sh_ref[0], sc_ref[0]).astype(o_ref.dtype)


def _prenorm2(x2d, norm_g, mod3, batch, shift_blk, scale_blk):
    n_lat, d = x2d.shape
    tr = 256
    tiles_per_batch = n_lat // tr // batch
    return pl.pallas_call(
        _prenorm2_kernel,
        grid=(n_lat // tr,),
        in_specs=[pl.BlockSpec((tr, d), lambda i: (i, 0)),
                  pl.BlockSpec((1, d), lambda i: (0, 0)),
                  pl.BlockSpec((1, 1, d), lambda i: (i // tiles_per_batch, 0, shift_blk)),
                  pl.BlockSpec((1, 1, d), lambda i: (i // tiles_per_batch, 0, scale_blk))],
        out_specs=pl.BlockSpec((tr, d), lambda i: (i, 0)),
        out_shape=jax.ShapeDtypeStruct((n_lat, d), BF16),
        compiler_params=_cparams(1),
        name="prenorm2",
    )(x2d, norm_g, mod3, mod3)


def _mm_kernel(*refs, n_a, which_a, n_extra, epilogue, w_t):
    n_w = len(which_a)
    a_refs = refs[:n_a]
    w_refs = refs[n_a:n_a + n_w]
    e_refs = refs[n_a + n_w:n_a + n_w + n_extra]
    o_ref = refs[n_a + n_w + n_extra]
    w_scr = refs[n_a + n_w + n_extra + 1:]

    @pl.when(pl.program_id(1) == 0)
    def _():
        for k in range(n_w):
            w_scr[k][...] = w_refs[k][...].astype(BF16)

    dims = (((1,), (1,)), ((), ())) if w_t else (((1,), (0,)), ((), ()))
    accs = [lax.dot_general(a_refs[which_a[k]][...], w_scr[k][...], dims, preferred_element_type=F32)
            for k in range(n_w)]
    o_ref[...] = epilogue(accs, e_refs).astype(o_ref.dtype)


def _matmul(a_list, w_list, extras, epilogue, n_out, out_dtype, *, m_rows, tm, tn, name, w_t=False):
    grid = (n_out // tn, m_rows // tm)
    in_specs, args, scratch = [], [], []
    for a, off in a_list:
        in_specs.append(pl.BlockSpec((tm, a.shape[1]), lambda j, m, off=off: (m + off, 0)))
        args.append(a)
    for w, off, _ in w_list:
        if w_t:
            in_specs.append(pl.BlockSpec((pl.Element(tn), pl.Element(w.shape[1])),
                                         lambda j, m, off=off: (pl.multiple_of(off + j * tn, 8), 0)))
            scratch.append(pltpu.VMEM((tn, w.shape[1]), BF16))
        else:
            in_specs.append(pl.BlockSpec((w.shape[0], tn), lambda j, m, off=off: (0, j + off)))
            scratch.append(pltpu.VMEM((w.shape[0], tn), BF16))
        args.append(w)
    for e, blk, imap in extras:
        in_specs.append(pl.BlockSpec(blk, imap))
        args.append(e)
    kern = functools.partial(_mm_kernel, n_a=len(a_list), which_a=tuple(w[2] for w in w_list),
                             n_extra=len(extras), epilogue=epilogue, w_t=w_t)
    return pl.pallas_call(
        kern,
        grid=grid,
        in_specs=in_specs,
        out_specs=pl.BlockSpec((tm, tn), lambda j, m: (m, j)),
        out_shape=jax.ShapeDtypeStruct((m_rows, n_out), out_dtype),
        scratch_shapes=scratch,
        compiler_params=_cparams(2),
        name=name,
    )(*args)


def _ep_plain(accs, e_refs):
    return accs[0]


def _ep_swiglu(accs, e_refs):
    return _silu(accs[0]) * accs[1]


def _ep_merge(accs, e_refs):
    g_ssm = e_refs[0][...].astype(F32)
    g_attn = e_refs[1][...].astype(F32)
    return jax.nn.sigmoid(g_ssm) * accs[0] + jax.nn.sigmoid(g_attn) * accs[1]


def _ep_gated_residual(accs, e_refs):
    return e_refs[0][...] + e_refs[1][0] * accs[0]


def _dt_kernel(w_ref, h_ref, b_ref, o_ref):
    raw = lax.dot_general(w_ref[...], h_ref[...], (((1,), (1,)), ((), ())),
                          preferred_element_type=F32) + b_ref[...]
    dt = jnp.maximum(raw, 0.0) + jnp.log1p(jnp.exp(-jnp.abs(raw)))
    for c in range(o_ref.shape[0]):
        o_ref[c] = dt[:, c * CHUNK:(c + 1) * CHUNK]


def _dt_proj(w_dt_t, h_all, bias_col):
    n_rows, d = h_all.shape
    n_dt = w_dt_t.shape[0]
    tm = 512
    return pl.pallas_call(
        _dt_kernel,
        grid=(n_rows // tm,),
        in_specs=[pl.BlockSpec((n_dt, d), lambda m: (0, 0)),
                  pl.BlockSpec((tm, d), lambda m: (m, 0)),
                  pl.BlockSpec((n_dt, 1), lambda m: (0, 0))],
        out_specs=pl.BlockSpec((tm // CHUNK, n_dt, CHUNK), lambda m: (m, 0, 0)),
        out_shape=jax.ShapeDtypeStruct((n_rows // CHUNK, n_dt, CHUNK), F32),
        compiler_params=_cparams(1),
        name="dt_proj",
    )(w_dt_t, h_all, bias_col)


def _qk_prep_kernel(t_ref, g_ref, cos_ref, sin_ref, o_ref, *, n_heads, scale):
    cs = cos_ref[...]
    sn = sin_ref[...]
    lane = lax.broadcasted_iota(jnp.int32, cs.shape, 1)
    first_half = (lane % (HEAD_DIM // 2)) < (HEAD_DIM // 4)
    g = g_ref[...]
    for h in range(n_heads):
        t = t_ref[:, h * HEAD_DIM:(h + 1) * HEAD_DIM].astype(F32)
        tn = (t * lax.rsqrt(jnp.mean(t * t, axis=-1, keepdims=True) + EPS)) * g
        rot = jnp.where(first_half,
                        pltpu.roll(tn, HEAD_DIM - HEAD_DIM // 4, 1),
                        pltpu.roll(tn, HEAD_DIM // 4, 1))
        o_ref[:, h * HEAD_DIM:(h + 1) * HEAD_DIM] = ((tn * cs + rot * sn) * scale).astype(o_ref.dtype)


def _qk_prep(t, n_rows, n_heads, norm_g, cos_t, sin_t, table_block, scale, name):
    tm = 512
    width = n_heads * HEAD_DIM
    return pl.pallas_call(
        functools.partial(_qk_prep_kernel, n_heads=n_heads, scale=scale),
        grid=(n_rows // tm,),
        in_specs=[pl.BlockSpec((tm, width), lambda m: (m, 0)),
                  pl.BlockSpec((1, HEAD_DIM), lambda m: (0, 0)),
                  pl.BlockSpec((tm, HEAD_DIM), lambda m: (table_block(m), 0)),
                  pl.BlockSpec((tm, HEAD_DIM), lambda m: (table_block(m), 0))],
        out_specs=pl.BlockSpec((tm, width), lambda m: (m, 0)),
        out_shape=jax.ShapeDtypeStruct((n_rows, width), BF16),
        compiler_params=_cparams(1),
        name=name,
    )(t, norm_g, cos_t, sin_t)


def _attn_kernel(q_ref, kl_ref, kc_ref, vl_ref, vc_ref, o_ref, k_scr, v_scr, p0_scr, p1_scr, *, n_ctx, tq):
    k_scr[0:n_ctx, :] = kc_ref[...]
    k_scr[n_ctx:, :] = kl_ref[...]
    v_scr[0:n_ctx, 0:HEAD_DIM] = vc_ref[...]
    v_scr[n_ctx:, 0:HEAD_DIM] = vl_ref[...]
    v_scr[:, HEAD_DIM:] = jnp.ones((v_scr.shape[0], HEAD_DIM), BF16)
    n_tiles = q_ref.shape[0] // tq

    p_bufs = (p0_scr, p1_scr)

    def scores(t, r):
        r0 = pl.multiple_of(t * tq, tq)
        q = q_ref[pl.ds(r0, tq), r * HEAD_DIM:(r + 1) * HEAD_DIM]
        sc = lax.dot_general(q, k_scr[...], (((1,), (1,)), ((), ())), preferred_element_type=F32)
        mx = jnp.max(sc, axis=-1, keepdims=True)
        p_bufs[r % 2][...] = jnp.exp2((sc - mx).astype(BF16))

    def values(t, r):
        r0 = pl.multiple_of(t * tq, tq)
        o = jnp.dot(p_bufs[r % 2][...], v_scr[...], preferred_element_type=F32)
        den = o[:, HEAD_DIM:HEAD_DIM + 1]
        o_ref[pl.ds(r0, tq), r * HEAD_DIM:(r + 1) * HEAD_DIM] = (o[:, :HEAD_DIM] / den).astype(o_ref.dtype)

    def tile(t, first):
        for r in range(Q_REP):
            scores(t, r)
            if r > 0:
                values(t, r - 1)
            elif not first:
                values(t - 1, Q_REP - 1)

    tile(0, True)

    def body(t, carry):
        tile(t, False)
        return carry

    lax.fori_loop(1, n_tiles, body, 0)
    values(n_tiles - 1, Q_REP - 1)


def _attention(q, k_all, kv_all, batch, seq, n_ctx):
    tq = 256
    assert Q_REP % 2 == 0
    ctx_blk0 = batch * seq // n_ctx
    gw = Q_REP * HEAD_DIM
    n_keys = n_ctx + seq
    return pl.pallas_call(
        functools.partial(_attn_kernel, n_ctx=n_ctx, tq=tq),
        grid=(batch, KV_HEADS),
        in_specs=[pl.BlockSpec((seq, gw), lambda b, g: (b, g)),
                  pl.BlockSpec((seq, HEAD_DIM), lambda b, g: (b, g)),
                  pl.BlockSpec((n_ctx, HEAD_DIM), lambda b, g: (ctx_blk0 + b, g)),
                  pl.BlockSpec((seq, HEAD_DIM), lambda b, g: (b, KV_HEADS + g)),
                  pl.BlockSpec((n_ctx, HEAD_DIM), lambda b, g: (ctx_blk0 + b, KV_HEADS + g))],
        out_specs=pl.BlockSpec((seq, gw), lambda b, g: (b, g)),
        out_shape=jax.ShapeDtypeStruct((batch * seq, ATTN_HEADS * HEAD_DIM), BF16),
        scratch_shapes=[pltpu.VMEM((n_keys, HEAD_DIM), BF16),
                        pltpu.VMEM((n_keys, 2 * HEAD_DIM), BF16),
                        pltpu.VMEM((tq, n_keys), BF16),
                        pltpu.VMEM((tq, n_keys), BF16)],
        compiler_params=_cparams(2),
        name="attention",
    )(q, k_all, k_all, kv_all, kv_all)


def _split3(v):
    hi = v.astype(BF16).astype(F32)
    r1 = v - hi
    mid = r1.astype(BF16).astype(F32)
    lo = (r1 - mid).astype(BF16).astype(F32)
    return hi, mid, lo


CONV_HALO = 16


def _conv_silu(src_ref, n_rows, w_ref, b_ref, store):
    n_blk = n_rows // CHUNK
    w = w_ref[...]
    bias = b_ref[...]
    pad = (CONV_W - 1) // 2

    def body(i, carry):
        r0 = pl.multiple_of(i * CHUNK, CHUNK)
        cur = src_ref[pl.ds(r0, CHUNK), :].astype(F32)
        prev = src_ref[pl.ds(pl.multiple_of(jnp.maximum(r0 - CONV_HALO, 0), CONV_HALO), CONV_HALO), :]
        nxt = src_ref[pl.ds(pl.multiple_of(jnp.minimum(r0 + CHUNK, n_rows - CONV_HALO), CONV_HALO),
                            CONV_HALO), :]
        prev = jnp.where(i > 0, prev.astype(F32), 0.0)
        nxt = jnp.where(i < n_blk - 1, nxt.astype(F32), 0.0)
        cat = jnp.concatenate([prev, cur, nxt], axis=0)
        acc = jnp.broadcast_to(bias, cur.shape)
        for j in range(CONV_W):
            acc = acc + w[j:j + 1, :] * cat[CONV_HALO - pad + j:CONV_HALO - pad + j + CHUNK, :]
        store(r0, _silu(acc))
        return carry

    lax.fori_loop(0, n_blk, body, 0)


def _ssd_kernel(x_lat, b_lat, c_lat, x_ctx, b_ctx, c_ctx,
                cwx, cwb, cwc, cbx, cbb, cbc,
                dt_lat, dt_ctx, aneg_ref, dskip_ref, z_ref, ng_ref,
                o_ref,
                xc, bt, cc, y_scr, dt_scr, u_scr, xt_scr, st_f, st_b, *, seq, n_ctx):
    nh = HEADS_PER_GROUP
    n_ctx_chunks = n_ctx // CHUNK
    n_lat_chunks = seq // CHUNK
    n_chunks = n_ctx_chunks + n_lat_chunks

    def store_x(off):
        def f(r0, v):
            xc[pl.ds(pl.multiple_of(off + r0, CHUNK), CHUNK), :] = v
        return f

    def store_c(off):
        def f(r0, v):
            cc[pl.ds(pl.multiple_of(off + r0, CHUNK), CHUNK), :] = v.astype(BF16)
        return f

    def store_bt(off):
        def f(r0, v):
            bt[:, pl.ds(pl.multiple_of(off + r0, CHUNK), CHUNK)] = v.T.astype(BF16)
        return f

    _conv_silu(x_ctx, n_ctx, cwx, cbx, store_x(0))
    _conv_silu(b_ctx, n_ctx, cwb, cbb, store_bt(0))
    _conv_silu(c_ctx, n_ctx, cwc, cbc, store_c(0))
    _conv_silu(x_lat, seq, cwx, cbx, store_x(n_ctx))
    _conv_silu(b_lat, seq, cwb, cbb, store_bt(n_ctx))
    _conv_silu(c_lat, seq, cwc, cbc, store_c(n_ctx))

    row = lax.broadcasted_iota(jnp.int32, (CHUNK, CHUNK), 0)
    col = lax.broadcasted_iota(jnp.int32, (CHUNK, CHUNK), 1)
    triu = (row <= col).astype(BF16)
    eye = (row == col).astype(BF16)
    lower = row >= col
    upper = row <= col
    dskip = dskip_ref[...]
    lane_head = lax.broadcasted_iota(jnp.int32, (CHUNK, GROUP_CH), 1) // SSM_HEAD_DIM

    def spread(n_cols, q_of_col, hd_of_col):
        k = lax.broadcasted_iota(jnp.int32, (CHUNK, n_cols), 0)
        c = lax.broadcasted_iota(jnp.int32, (CHUNK, n_cols), 1)
        return ((k < 72) & (k // 24 == q_of_col(c)) & (k % 8 == hd_of_col(c))).astype(BF16)

    n_cb = 2 * nh * CHUNK
    rmat_f = spread(n_cb + 2 * GROUP_CH,
                    lambda c: jnp.where(c < n_cb, 0, jnp.where(c < n_cb + GROUP_CH, 1, 2)),
                    lambda c: jnp.where(c < n_cb, c // CHUNK, ((c - n_cb) % GROUP_CH) // SSM_HEAD_DIM))
    rmat_b = spread(2 * GROUP_CH,
                    lambda c: jnp.where(c < GROUP_CH, 1, 2),
                    lambda c: nh + (c % GROUP_CH) // SSM_HEAD_DIM)

    dt_scr[0:n_ctx_chunks] = dt_ctx[...]
    dt_scr[n_ctx_chunks:n_chunks] = dt_lat[...]
    dt3 = dt_scr[...]
    n_rows = n_chunks * 8
    dt2 = dt3.reshape(n_rows, CHUNK)
    a2 = (dt3 * aneg_ref[...][None]).reshape(n_rows, CHUNK)
    cum = sum(jnp.dot(part.astype(BF16), triu, preferred_element_type=F32) for part in _split3(a2))
    total = jnp.broadcast_to(cum[:, CHUNK - 1:CHUNK], cum.shape)
    cumx = cum - a2
    is_fwd = (lax.broadcasted_iota(jnp.int32, (n_rows, CHUNK), 0) % 8) < nh
    u2 = jnp.where(is_fwd, cum, cumx)
    w2 = dt2 * jnp.exp(jnp.where(is_fwd, total - cum, cumx))
    e2 = jnp.exp(jnp.where(is_fwd, cum, total - cumx))
    u_scr[...] = u2.reshape(n_chunks, 8, CHUNK)
    pieces = [p.reshape(n_chunks, 8, CHUNK) for v in (u2, w2, e2) for p in _split3(v)]
    pieces.append(jnp.zeros((n_chunks, CHUNK - 8 * len(pieces), CHUNK), F32))
    packed = jnp.concatenate(pieces, axis=1).reshape(n_chunks * CHUNK, CHUNK).astype(BF16)
    xt_scr[...] = lax.dot_general(eye, packed, (((1,), (1,)), ((), ())),
                                  preferred_element_type=F32).astype(BF16)

    def fwd_chunk(ci, y_off_row, st_prev):
        r0 = pl.multiple_of(ci * CHUNK, CHUNK)
        xt = xt_scr[:, pl.ds(r0, CHUNK)]
        xs = xc[pl.ds(r0, CHUNK), :]
        btc = bt[:, pl.ds(r0, CHUNK)]
        if y_off_row is None:
            big = jnp.dot(xt, rmat_f[:, n_cb:], preferred_element_type=F32)
            w_col, e_col = big[:, :GROUP_CH], big[:, GROUP_CH:]
        else:
            big = jnp.dot(xt, rmat_f, preferred_element_type=F32)
            cb, w_col, e_col = big[:, :n_cb], big[:, n_cb:n_cb + GROUP_CH], big[:, n_cb + GROUP_CH:]
            u8 = u_scr[ci]
            dt8 = dt_scr[ci]
            cm = cc[pl.ds(r0, CHUNK), :]
            g = jnp.dot(cm, btc, preferred_element_type=F32)
            xs_b = xs.astype(BF16)
            w_parts, x_parts = [], []
            for r in range(nh):
                ef = cb[:, r * CHUNK:(r + 1) * CHUNK] - u8[r:r + 1, :]
                eb = u8[nh + r:nh + r + 1, :] - cb[:, (nh + r) * CHUNK:(nh + r + 1) * CHUNK]
                df = jnp.exp(jnp.where(lower, ef, NEG_BIG)) * dt8[r:r + 1, :]
                db = jnp.exp(jnp.where(upper, eb, NEG_BIG)) * dt8[nh + r:nh + r + 1, :]
                w_parts.append((g * (df + db)).astype(BF16))
                x_parts.append(jnp.where(lane_head == r, xs_b, jnp.zeros_like(xs_b)))
            y = jnp.dot(jnp.concatenate(w_parts, axis=1), jnp.concatenate(x_parts, axis=0),
                        preferred_element_type=F32)
            y = y + e_col * jnp.dot(cm, st_prev.astype(BF16), preferred_element_type=F32)
            y_scr[pl.ds(y_off_row, CHUNK), :] = y + dskip * xs
        xw = (xs * w_col).astype(BF16)
        return st_prev * e_col[CHUNK - 1:CHUNK, :] + jnp.dot(btc, xw, preferred_element_type=F32)

    def bwd_chunk(ci, y_off_row, st_prev):
        r0 = pl.multiple_of(ci * CHUNK, CHUNK)
        xt = xt_scr[:, pl.ds(r0, CHUNK)]
        xs = xc[pl.ds(r0, CHUNK), :]
        btc = bt[:, pl.ds(r0, CHUNK)]
        big = jnp.dot(xt, rmat_b, preferred_element_type=F32)
        w_col, e_col = big[:, :GROUP_CH], big[:, GROUP_CH:]
        if y_off_row is not None:
            cm = cc[pl.ds(r0, CHUNK), :]
            y = y_scr[pl.ds(y_off_row, CHUNK), :]
            y = y + e_col * jnp.dot(cm, st_prev.astype(BF16), preferred_element_type=F32)
            gated = y * _silu(z_ref[pl.ds(y_off_row, CHUNK), :].astype(F32))
            ms = jnp.mean(gated * gated, axis=-1, keepdims=True)
            o_ref[pl.ds(y_off_row, CHUNK), :] = ((gated * lax.rsqrt(ms + EPS)) * ng_ref[...]).astype(o_ref.dtype)
        xw = (xs * w_col).astype(BF16)
        return st_prev * e_col[0:1, :] + jnp.dot(btc, xw, preferred_element_type=F32)

    unroll = SSD_UNROLL if n_lat_chunks % SSD_UNROLL == 0 else 1

    st = jnp.zeros((D_STATE, GROUP_CH), F32)
    for c in range(n_ctx_chunks):
        st = fwd_chunk(c, None, st)
    st_f[...] = st

    def fwd_body(i, carry):
        s = st_f[...]
        for u in range(unroll):
            c = i * unroll + u
            s = fwd_chunk(n_ctx_chunks + c, pl.multiple_of(c * CHUNK, CHUNK), s)
        st_f[...] = s
        return carry

    lax.fori_loop(0, n_lat_chunks // unroll, fwd_body, 0)

    st = jnp.zeros((D_STATE, GROUP_CH), F32)
    for c in reversed(range(n_ctx_chunks)):
        st = bwd_chunk(c, None, st)
    st_b[...] = st

    def bwd_body(i, carry):
        s = st_b[...]
        for u in range(unroll):
            c = n_lat_chunks - 1 - (i * unroll + u)
            s = bwd_chunk(n_ctx_chunks + c, pl.multiple_of(c * CHUNK, CHUNK), s)
        st_b[...] = s
        return carry

    lax.fori_loop(0, n_lat_chunks // unroll, bwd_body, 0)


def _ssd(xbc_all, conv_w, conv_b, dt_c, aneg_col, dskip_row, z, norm_g, batch, seq, n_ctx):
    d_ssm = SSM_HEADS * SSM_HEAD_DIM
    b0 = d_ssm // D_STATE
    c0 = b0 + SSM_GROUPS
    ctx_blk0 = batch * seq // n_ctx
    n_ctx_chunks, n_lat_chunks = n_ctx // CHUNK, seq // CHUNK
    n_tok = n_ctx + seq
    in_specs = [
        pl.BlockSpec((seq, GROUP_CH), lambda b, g: (b, g)),
        pl.BlockSpec((seq, D_STATE), lambda b, g: (b, b0 + g)),
        pl.BlockSpec((seq, D_STATE), lambda b, g: (b, c0 + g)),
        pl.BlockSpec((n_ctx, GROUP_CH), lambda b, g: (ctx_blk0 + b, g)),
        pl.BlockSpec((n_ctx, D_STATE), lambda b, g: (ctx_blk0 + b, b0 + g)),
        pl.BlockSpec((n_ctx, D_STATE), lambda b, g: (ctx_blk0 + b, c0 + g)),
        pl.BlockSpec((CONV_W, GROUP_CH), lambda b, g: (0, g)),
        pl.BlockSpec((CONV_W, D_STATE), lambda b, g: (0, b0 + g)),
        pl.BlockSpec((CONV_W, D_STATE), lambda b, g: (0, c0 + g)),
        pl.BlockSpec((1, GROUP_CH), lambda b, g: (0, g)),
        pl.BlockSpec((1, D_STATE), lambda b, g: (0, b0 + g)),
        pl.BlockSpec((1, D_STATE), lambda b, g: (0, c0 + g)),
        pl.BlockSpec((n_lat_chunks, 8, CHUNK), lambda b, g: (b, g, 0)),
        pl.BlockSpec((n_ctx_chunks, 8, CHUNK), lambda b, g: (ctx_blk0 + b, g, 0)),
        pl.BlockSpec((8, 1), lambda b, g: (g, 0)),
        pl.BlockSpec((1, GROUP_CH), lambda b, g: (0, g)),
        pl.BlockSpec((seq, GROUP_CH), lambda b, g: (b, g)),
        pl.BlockSpec((1, GROUP_CH), lambda b, g: (0, g)),
    ]
    return pl.pallas_call(
        functools.partial(_ssd_kernel, seq=seq, n_ctx=n_ctx),
        grid=(batch, SSM_GROUPS),
        in_specs=in_specs,
        out_specs=pl.BlockSpec((seq, GROUP_CH), lambda b, g: (b, g)),
        out_shape=jax.ShapeDtypeStruct((batch * seq, d_ssm), BF16),
        scratch_shapes=[pltpu.VMEM((n_tok, GROUP_CH), F32),
                        pltpu.VMEM((D_STATE, n_tok), BF16),
                        pltpu.VMEM((n_tok, D_STATE), BF16),
                        pltpu.VMEM((seq, GROUP_CH), F32),
                        pltpu.VMEM((n_tok // CHUNK, 8, CHUNK), F32),
                        pltpu.VMEM((n_tok // CHUNK, 8, CHUNK), F32),
                        pltpu.VMEM((CHUNK, n_tok), BF16),
                        pltpu.VMEM((D_STATE, GROUP_CH), F32),
                        pltpu.VMEM((D_STATE, GROUP_CH), F32)],
        compiler_params=_cparams(2),
        name="ssd",
    )(xbc_all, xbc_all, xbc_all, xbc_all, xbc_all, xbc_all,
      conv_w, conv_w, conv_w, conv_b, conv_b, conv_b,
      dt_c, dt_c, aneg_col, dskip_row, z, norm_g)


def _rope_tables(seq):
    n_freq = HEAD_DIM // 4
    pos = jnp.arange(seq)
    rows = (pos // GRID_W).astype(F32)
    cols = (pos % GRID_W).astype(F32)
    inv = jnp.power(ROPE_THETA, -jnp.arange(n_freq, dtype=F32) / n_freq)
    ang_r = rows[:, None] * inv
    ang_c = cols[:, None] * inv
    cos_t = jnp.concatenate([jnp.cos(ang_r), jnp.cos(ang_r), jnp.cos(ang_c), jnp.cos(ang_c)], axis=1)
    sin_t = jnp.concatenate([-jnp.sin(ang_r), jnp.sin(ang_r), -jnp.sin(ang_c), jnp.sin(ang_c)], axis=1)
    return cos_t, sin_t


def kernel(x, c, ctx, c_ctx, w_mod, b_mod, norm1, w_in, conv_w, conv_b, dt_bias, a_log, d_skip,
           ssm_norm, q_norm, k_norm, w_ssm_br, w_attn_br, w_o, norm2, w_ffn_in, w_ffn_out):
    batch, seq, d = x.shape
    n_ctx = ctx.shape[1]
    n_lat = batch * seq
    n_ctx_rows = batch * n_ctx
    n_all = n_lat + n_ctx_rows
    d_ssm = SSM_HEADS * SSM_HEAD_DIM
    d_conv = d_ssm + 2 * SSM_GROUPS * D_STATE
    d_attn = ATTN_HEADS * HEAD_DIM
    d_kv = KV_HEADS * HEAD_DIM
    d_ff = w_ffn_out.shape[1]
    n_dt = 2 * SSM_HEADS
    o_xbc = d_ssm
    o_dt = o_xbc + d_conv
    o_q = o_dt + n_dt
    o_k = o_q + d_attn
    o_g = o_k + 2 * d_kv
    assert w_mod.shape[0] == 1, "single-layer block"

    cvec = jnp.concatenate([c, c_ctx[None, :], jnp.zeros((8 - batch - 1, d), F32)], axis=0)
    mod = _modulation(cvec, w_mod[0], b_mod[0][None, :])
    mod3 = mod[:batch + 1].reshape(batch + 1, 1, N_MOD * d)

    x2d = x.reshape(n_lat, d)
    h_all = _prenorm1(x2d, ctx.reshape(n_ctx_rows, d), norm1, mod3, batch, 0, 1)

    w_in_t = jnp.transpose(w_in[0])
    tn = 1024
    tm_lat = 1024 if seq % 1024 == 0 else 512
    tm_all = n_all // 8 if (n_all // 8) % 16 == 0 and n_all % 8 == 0 else 512

    def in_proj(row_off, width, rows, tm_rows, out_dtype, name):
        return _matmul([(h_all, 0)], [(w_in_t, row_off, 0)], [], _ep_plain, width, out_dtype,
                       m_rows=rows, tm=tm_rows, tn=tn, name=name, w_t=True)

    z = in_proj(0, d_ssm, n_lat, tm_lat, BF16, "proj_z")
    xbc_all = in_proj(o_xbc, d_conv, n_all, tm_all, BF16, "proj_xbc")
    q_raw = in_proj(o_q, d_attn, n_lat, tm_lat, BF16, "proj_q")
    kv_all = in_proj(o_k, 2 * d_kv, n_all, tm_all, BF16, "proj_kv")
    gates = in_proj(o_g, 2 * d, n_lat, tm_lat, BF16, "proj_gates")

    perm = np.array([dr * SSM_HEADS + g * HEADS_PER_GROUP + r
                     for g in range(SSM_GROUPS) for dr in range(2) for r in range(HEADS_PER_GROUP)])
    w_dt_t = w_in_t[o_dt:o_dt + n_dt][perm].astype(BF16)
    dt_bias_col = dt_bias[0].reshape(n_dt)[perm][:, None]
    aneg_col = (-jnp.exp(a_log[0].astype(F32))).reshape(n_dt)[perm][:, None]
    dt_c = _dt_proj(w_dt_t, h_all, dt_bias_col)

    dskip_row = jnp.repeat(d_skip[0].astype(F32), SSM_HEAD_DIM)[None, :]
    y_norm = _ssd(xbc_all, conv_w[0], conv_b[0][None, :], dt_c, aneg_col, dskip_row, z,
                  ssm_norm, batch, seq, n_ctx)

    cos_t, sin_t = _rope_tables(seq)
    tq = 512
    seq_tiles = seq // tq
    q = _qk_prep(q_raw, n_lat, ATTN_HEADS, q_norm, cos_t, sin_t,
                 lambda m: m % seq_tiles, HEAD_DIM ** -0.5 * LOG2_E, "q_prep")
    cos_k = jnp.concatenate([cos_t, jnp.ones((tq, HEAD_DIM), F32)], axis=0)
    sin_k = jnp.concatenate([sin_t, jnp.zeros((tq, HEAD_DIM), F32)], axis=0)
    lat_tiles = n_lat // tq
    k_all = _qk_prep(kv_all, n_all, KV_HEADS, k_norm, cos_k, sin_k,
                     lambda m: jnp.where(m < lat_tiles, m % seq_tiles, seq_tiles), 1.0, "k_prep")
    attn = _attention(q, k_all, kv_all, batch, seq, n_ctx)

    tn_m = 512
    merged = _matmul([(y_norm, 0), (attn, 0)], [(w_ssm_br[0], 0, 0), (w_attn_br[0], 0, 1)],
                     [(gates, (tm_lat, tn_m), lambda j, m: (m, j)),
                      (gates, (tm_lat, tn_m), lambda j, m: (m, j + d // tn_m))],
                     _ep_merge, d, BF16, m_rows=n_lat, tm=tm_lat, tn=tn_m, name="branch_merge")
    tiles_per_batch = seq // tm_lat
    x_mid = _matmul([(merged, 0)], [(w_o[0], 0, 0)],
                    [(x2d, (tm_lat, tn), lambda j, m: (m, j)),
                     (mod3, (1, 1, tn), lambda j, m: (m // tiles_per_batch, 0, 2 * (d // tn) + j))],
                    _ep_gated_residual, d, F32, m_rows=n_lat, tm=tm_lat, tn=tn, name="out_proj")

    h2 = _prenorm2(x_mid, norm2, mod3, batch, 3, 4)
    tn_f = 512
    act = _matmul([(h2, 0)], [(w_ffn_in[0], 0, 0), (w_ffn_in[0], d_ff // tn_f, 0)], [], _ep_swiglu,
                  d_ff, BF16, m_rows=n_lat, tm=tm_lat, tn=tn_f, name="ffn_in")
    tm_o = 512
    tiles_per_batch_o = seq // tm_o
    out = _matmul([(act, 0)], [(w_ffn_out[0], 0, 0)],
                  [(x_mid, (tm_o, tn_f), lambda j, m: (m, j)),
                   (mod3, (1, 1, tn_f), lambda j, m: (m // tiles_per_batch_o, 0, 5 * (d // tn_f) + j))],
                  _ep_gated_residual, d, F32, m_rows=n_lat, tm=tm_o, tn=tn_f, name="ffn_out")
    return out.reshape(batch, seq, d)
```

```python
import functools
import math

import jax
import jax.numpy as jnp
import numpy as np
from jax import lax
from jax.experimental import pallas as pl
from jax.experimental.pallas import tpu as pltpu

F32 = jnp.float32
BF16 = jnp.bfloat16

GRID_W = 64
SSM_HEADS = 32
SSM_HEAD_DIM = 64
SSM_GROUPS = 8
D_STATE = 128
CONV_W = 5
CHUNK = 128
ATTN_HEADS = 16
KV_HEADS = 4
HEAD_DIM = 128
ROPE_THETA = 10000.0
N_MOD = 6
EPS = 1e-6

HEADS_PER_GROUP = SSM_HEADS // SSM_GROUPS
GROUP_CH = HEADS_PER_GROUP * SSM_HEAD_DIM
Q_REP = ATTN_HEADS // KV_HEADS

VMEM_LIMIT_BYTES = 56 * 1024 * 1024
NEG_BIG = -1e30
LOG2_E = 1.4426950408889634
SSD_UNROLL = 8


def _cparams(n_grid):
    return pltpu.CompilerParams(dimension_semantics=("arbitrary",) * n_grid,
                                vmem_limit_bytes=VMEM_LIMIT_BYTES)


def _silu(v):
    return v * jax.nn.sigmoid(v)


def _mod_kernel(c_ref, w_ref, b_ref, o_ref):
    s = _silu(c_ref[...]).astype(BF16)
    o_ref[...] = jnp.dot(s, w_ref[...].astype(BF16), preferred_element_type=F32) + b_ref[...]


def _modulation(cvec, w_mod, b_mod):
    rows, d = cvec.shape
    n = w_mod.shape[1]
    tn = 1024
    return pl.pallas_call(
        _mod_kernel,
        grid=(n // tn,),
        in_specs=[pl.BlockSpec((rows, d), lambda j: (0, 0)),
                  pl.BlockSpec((d, tn), lambda j: (0, j)),
                  pl.BlockSpec((1, tn), lambda j: (0, j))],
        out_specs=pl.BlockSpec((rows, tn), lambda j: (0, j)),
        out_shape=jax.ShapeDtypeStruct((rows, n), F32),
        compiler_params=_cparams(1),
        name="modulation",
    )(cvec, w_mod, b_mod)


PRENORM_ROWS = 512


def _norm_mod(t, g, sh, sc):
    ms = jnp.mean(t * t, axis=-1, keepdims=True)
    return ((t * lax.rsqrt(ms + EPS)) * g) * (1.0 + sc) + sh


def _prenorm1_kernel(x_ref, ctx_ref, g_ref, sh_ref, sc_ref, o_ref, *, n_lat_tiles):
    i = pl.program_id(0)

    @pl.when(i < n_lat_tiles)
    def _():
        o_ref[...] = _norm_mod(x_ref[...], g_ref[...], sh_ref[0], sc_ref[0]).astype(o_ref.dtype)

    @pl.when(i >= n_lat_tiles)
    def _():
        o_ref[...] = _norm_mod(ctx_ref[...], g_ref[...], sh_ref[0], sc_ref[0]).astype(o_ref.dtype)


def _prenorm1(x2d, ctx2d, norm_g, mod3, batch, shift_blk, scale_blk):
    n_lat, d = x2d.shape
    n_ctx = ctx2d.shape[0]
    tr = math.gcd(PRENORM_ROWS, math.gcd(n_lat // batch, n_ctx))
    lat_tiles, ctx_tiles = n_lat // tr, n_ctx // tr
    tiles_per_batch = lat_tiles // batch

    def mod_row(i):
        return jnp.where(i < lat_tiles, i // tiles_per_batch, batch)

    return pl.pallas_call(
        functools.partial(_prenorm1_kernel, n_lat_tiles=lat_tiles),
        grid=(lat_tiles + ctx_tiles,),
        in_specs=[pl.BlockSpec((tr, d), lambda i: (jnp.minimum(i, lat_tiles - 1), 0)),
                  pl.BlockSpec((tr, d), lambda i: (jnp.maximum(i - lat_tiles, 0), 0)),
                  pl.BlockSpec((1, d), lambda i: (0, 0)),
                  pl.BlockSpec((1, 1, d), lambda i: (mod_row(i), 0, shift_blk)),
                  pl.BlockSpec((1, 1, d), lambda i: (mod_row(i), 0, scale_blk))],
        out_specs=pl.BlockSpec((tr, d), lambda i: (i, 0)),
        out_shape=jax.ShapeDtypeStruct((n_lat + n_ctx, d), BF16),
        compiler_params=_cparams(1),
        name="prenorm1",
    )(x2d, ctx2d, norm_g, mod3, mod3)


def _prenorm2_kernel(x_ref, g_ref, sh_ref, sc_ref, o_ref):
    o_ref[...] = _norm_mod(x_ref[...], g_ref[...], sh_ref[0], sc_ref[0]).astype(o_ref.dtype)


def _prenorm2(x2d, norm_g, mod3, batch, shift_blk, scale_blk):
    n_lat, d = x2d.shape
    tr = math.gcd(PRENORM_ROWS, n_lat // batch)
    tiles_per_batch = n_lat // tr // batch
    return pl.pallas_call(
        _prenorm2_kernel,
        grid=(n_lat // tr,),
        in_specs=[pl.BlockSpec((tr, d), lambda i: (i, 0)),
                  pl.BlockSpec((1, d), lambda i: (0, 0)),
                  pl.BlockSpec((1, 1, d), lambda i: (i // tiles_per_batch, 0, shift_blk)),
                  pl.BlockSpec((1, 1, d), lambda i: (i // tiles_per_batch, 0, scale_blk))],
        out_specs=pl.BlockSpec((tr, d), lambda i: (i, 0)),
        out_shape=jax.ShapeDtypeStruct((n_lat, d), BF16),
        compiler_params=_cparams(1),
        name="prenorm2",
    )(x2d, norm_g, mod3, mod3)


def _mm_kernel(*refs, n_a, which_a, n_extra, epilogue, w_t):
    n_w = len(which_a)
    a_refs = refs[:n_a]
    w_refs = refs[n_a:n_a + n_w]
    e_refs = refs[n_a + n_w:n_a + n_w + n_extra]
    o_ref = refs[n_a + n_w + n_extra]
    w_scr = refs[n_a + n_w + n_extra + 1:]

    @pl.when(pl.program_id(1) == 0)
    def _():
        for k in range(n_w):
            w_scr[k][...] = w_refs[k][...].astype(BF16)

    dims = (((1,), (1,)), ((), ())) if w_t else (((1,), (0,)), ((), ()))
    accs = [lax.dot_general(a_refs[which_a[k]][...], w_scr[k][...], dims, preferred_element_type=F32)
            for k in range(n_w)]
    o_ref[...] = epilogue(accs, e_refs).astype(o_ref.dtype)


def _matmul(a_list, w_list, extras, epilogue, n_out, out_dtype, *, m_rows, tm, tn, name, w_t=False):
    grid = (n_out // tn, m_rows // tm)
    in_specs, args, scratch = [], [], []
    for a, off in a_list:
        in_specs.append(pl.BlockSpec((tm, a.shape[1]), lambda j, m, off=off: (m + off, 0)))
        args.append(a)
    for w, off, _ in w_list:
        if w_t:
            in_specs.append(pl.BlockSpec((pl.Element(tn), pl.Element(w.shape[1])),
                                         lambda j, m, off=off: (pl.multiple_of(off + j * tn, 8), 0)))
            scratch.append(pltpu.VMEM((tn, w.shape[1]), BF16))
        else:
            in_specs.append(pl.BlockSpec((w.shape[0], tn), lambda j, m, off=off: (0, j + off)))
            scratch.append(pltpu.VMEM((w.shape[0], tn), BF16))
        args.append(w)
    for e, blk, imap in extras:
        in_specs.append(pl.BlockSpec(blk, imap))
        args.append(e)
    kern = functools.partial(_mm_kernel, n_a=len(a_list), which_a=tuple(w[2] for w in w_list),
                             n_extra=len(extras), epilogue=epilogue, w_t=w_t)
    return pl.pallas_call(
        kern,
        grid=grid,
        in_specs=in_specs,
        out_specs=pl.BlockSpec((tm, tn), lambda j, m: (m, j)),
        out_shape=jax.ShapeDtypeStruct((m_rows, n_out), out_dtype),
        scratch_shapes=scratch,
        compiler_params=_cparams(2),
        name=name,
    )(*args)


def _ep_plain(accs, e_refs):
    return accs[0]


def _ep_swiglu(accs, e_refs):
    return _silu(accs[0]) * accs[1]


def _ep_merge(accs, e_refs):
    g_ssm = e_refs[0][...].astype(F32)
    g_attn = e_refs[1][...].astype(F32)
    return jax.nn.sigmoid(g_ssm) * accs[0] + jax.nn.sigmoid(g_attn) * accs[1]


def _ep_gated_residual(accs, e_refs):
    return e_refs[0][...] + e_refs[1][0] * accs[0]


def _dt_kernel(w_ref, h_ref, b_ref, o_ref):
    raw = lax.dot_general(w_ref[...], h_ref[...], (((1,), (1,)), ((), ())),
                          preferred_element_type=F32) + b_ref[...]
    dt = jnp.maximum(raw, 0.0) + jnp.log1p(jnp.exp(-jnp.abs(raw)))
    for c in range(o_ref.shape[0]):
        o_ref[c] = dt[:, c * CHUNK:(c + 1) * CHUNK]


def _dt_proj(w_dt_t, h_all, bias_col):
    n_rows, d = h_all.shape
    n_dt = w_dt_t.shape[0]
    tm = 512
    return pl.pallas_call(
        _dt_kernel,
        grid=(n_rows // tm,),
        in_specs=[pl.BlockSpec((n_dt, d), lambda m: (0, 0)),
                  pl.BlockSpec((tm, d), lambda m: (m, 0)),
                  pl.BlockSpec((n_dt, 1), lambda m: (0, 0))],
        out_specs=pl.BlockSpec((tm // CHUNK, n_dt, CHUNK), lambda m: (m, 0, 0)),
        out_shape=jax.ShapeDtypeStruct((n_rows // CHUNK, n_dt, CHUNK), F32),
        compiler_params=_cparams(1),
        name="dt_proj",
    )(w_dt_t, h_all, bias_col)


def _norm_rope(t, g, cs, sn, scale):
    tn = (t * lax.rsqrt(jnp.mean(t * t, axis=-1, keepdims=True) + EPS)) * g
    lane = lax.broadcasted_iota(jnp.int32, tn.shape, 1)
    first_half = (lane % (HEAD_DIM // 2)) < (HEAD_DIM // 4)
    rot = jnp.where(first_half,
                    pltpu.roll(tn, HEAD_DIM - HEAD_DIM // 4, 1),
                    pltpu.roll(tn, HEAD_DIM // 4, 1))
    return (tn * cs + rot * sn) * scale


def _attn_kernel(q_ref, kl_ref, kc_ref, vl_ref, vc_ref, qg_ref, kg_ref, cos_ref, sin_ref, o_ref,
                 k_scr, v_scr, s0_scr, s1_scr, p0_scr, p1_scr, *, n_ctx, tq, q_scale):
    seq = q_ref.shape[0]
    n_tiles = seq // tq
    kg = kg_ref[...]
    qg = qg_ref[...]

    kc = kc_ref[...].astype(F32)
    k_scr[0:n_ctx, :] = ((kc * lax.rsqrt(jnp.mean(kc * kc, axis=-1, keepdims=True) + EPS)) * kg).astype(BF16)

    def k_body(i, carry):
        r0 = pl.multiple_of(i * tq, tq)
        kt = _norm_rope(kl_ref[pl.ds(r0, tq), :].astype(F32), kg,
                        cos_ref[pl.ds(r0, tq), :], sin_ref[pl.ds(r0, tq), :], 1.0)
        k_scr[pl.ds(pl.multiple_of(n_ctx + r0, tq), tq), :] = kt.astype(BF16)
        return carry

    lax.fori_loop(0, n_tiles, k_body, 0)
    v_scr[0:n_ctx, 0:HEAD_DIM] = vc_ref[...]
    v_scr[n_ctx:, 0:HEAD_DIM] = vl_ref[...]
    v_scr[:, HEAD_DIM:] = jnp.ones((v_scr.shape[0], HEAD_DIM), BF16)

    s_bufs = (s0_scr, s1_scr)
    p_bufs = (p0_scr, p1_scr)

    def scores(t, r):
        r0 = pl.multiple_of(t * tq, tq)
        q = _norm_rope(q_ref[pl.ds(r0, tq), r * HEAD_DIM:(r + 1) * HEAD_DIM].astype(F32), qg,
                       cos_ref[pl.ds(r0, tq), :], sin_ref[pl.ds(r0, tq), :], q_scale).astype(BF16)
        s_bufs[r % 2][...] = lax.dot_general(q, k_scr[...], (((1,), (1,)), ((), ())),
                                             preferred_element_type=F32)

    def softmax(r):
        sc = s_bufs[r % 2][...]
        mx = jnp.max(sc, axis=-1, keepdims=True)
        p_bufs[r % 2][...] = jnp.exp2((sc - mx).astype(BF16))

    def values(t, r):
        r0 = pl.multiple_of(t * tq, tq)
        o = jnp.dot(p_bufs[r % 2][...], v_scr[...], preferred_element_type=F32)
        den = o[:, HEAD_DIM:HEAD_DIM + 1]
        o_ref[pl.ds(r0, tq), r * HEAD_DIM:(r + 1) * HEAD_DIM] = (o[:, :HEAD_DIM] / den).astype(o_ref.dtype)

    def tile(t, first):
        for r in range(Q_REP):
            if r >= 1:
                softmax(r - 1)
            elif not first:
                softmax(Q_REP - 1)
            if r >= 2:
                values(t, r - 2)
            elif not first:
                values(t - 1, Q_REP - 2 + r)
            scores(t, r)

    tile(0, True)

    def body(t, carry):
        tile(t, False)
        return carry

    lax.fori_loop(1, n_tiles, body, 0)
    softmax(Q_REP - 1)
    values(n_tiles - 1, Q_REP - 2)
    values(n_tiles - 1, Q_REP - 1)


def _attention(q_raw, kv_all, q_norm, k_norm, cos_t, sin_t, batch, seq, n_ctx):
    tq = 256
    assert Q_REP % 2 == 0 and Q_REP >= 2
    ctx_blk0 = batch * seq // n_ctx
    gw = Q_REP * HEAD_DIM
    n_keys = n_ctx + seq
    const = dict(pipeline_mode=pl.Buffered(1))
    return pl.pallas_call(
        functools.partial(_attn_kernel, n_ctx=n_ctx, tq=tq, q_scale=HEAD_DIM ** -0.5 * LOG2_E),
        grid=(batch, KV_HEADS),
        in_specs=[pl.BlockSpec((seq, gw), lambda b, g: (b, g)),
                  pl.BlockSpec((seq, HEAD_DIM), lambda b, g: (b, g)),
                  pl.BlockSpec((n_ctx, HEAD_DIM), lambda b, g: (ctx_blk0 + b, g)),
                  pl.BlockSpec((seq, HEAD_DIM), lambda b, g: (b, KV_HEADS + g)),
                  pl.BlockSpec((n_ctx, HEAD_DIM), lambda b, g: (ctx_blk0 + b, KV_HEADS + g)),
                  pl.BlockSpec((1, HEAD_DIM), lambda b, g: (0, 0)),
                  pl.BlockSpec((1, HEAD_DIM), lambda b, g: (0, 0)),
                  pl.BlockSpec((seq, HEAD_DIM), lambda b, g: (0, 0), **const),
                  pl.BlockSpec((seq, HEAD_DIM), lambda b, g: (0, 0), **const)],
        out_specs=pl.BlockSpec((seq, gw), lambda b, g: (b, g)),
        out_shape=jax.ShapeDtypeStruct((batch * seq, ATTN_HEADS * HEAD_DIM), BF16),
        scratch_shapes=[pltpu.VMEM((n_keys, HEAD_DIM), BF16),
                        pltpu.VMEM((n_keys, 2 * HEAD_DIM), BF16),
                        pltpu.VMEM((tq, n_keys), F32),
                        pltpu.VMEM((tq, n_keys), F32),
                        pltpu.VMEM((tq, n_keys), BF16),
                        pltpu.VMEM((tq, n_keys), BF16)],
        compiler_params=_cparams(2),
        name="attention",
    )(q_raw, kv_all, kv_all, kv_all, kv_all, q_norm, k_norm, cos_t, sin_t)


def _split3(v):
    hi = v.astype(BF16).astype(F32)
    r1 = v - hi
    mid = r1.astype(BF16).astype(F32)
    lo = (r1 - mid).astype(BF16).astype(F32)
    return hi, mid, lo


CONV_HALO = 16
CONV_UNROLL = 4


def _conv_shift_matrix():
    pad = (CONV_W - 1) // 2
    taps = [j for j in range(CONV_W) if j != pad]
    rows = lax.broadcasted_iota(jnp.int32, (len(taps) * CHUNK, CHUNK + 2 * CONV_HALO), 0)
    cols = lax.broadcasted_iota(jnp.int32, (len(taps) * CHUNK, CHUNK + 2 * CONV_HALO), 1)
    tap = jnp.where(rows // CHUNK < pad, rows // CHUNK, rows // CHUNK + 1)
    return (cols == CONV_HALO - pad + tap + rows % CHUNK).astype(BF16), taps


def _conv_silu(streams, n_rows, shift, taps, unroll):
    n_blk = n_rows // CHUNK
    pad = (CONV_W - 1) // 2
    unroll = math.gcd(unroll, n_blk)

    def lanes(parts):
        return parts[0] if len(parts) == 1 else jnp.concatenate(parts, axis=1)

    def block(i):
        r0 = pl.multiple_of(i * CHUNK, CHUNK)
        lo = pl.multiple_of(jnp.maximum(r0 - CONV_HALO, 0), CONV_HALO)
        hi = pl.multiple_of(jnp.minimum(r0 + CHUNK, n_rows - CONV_HALO), CONV_HALO)
        for src_refs, w_refs, b_refs, store in streams:
            cur = lanes([s[pl.ds(r0, CHUNK), :] for s in src_refs])
            prev = lanes([s[pl.ds(lo, CONV_HALO), :] for s in src_refs])
            nxt = lanes([s[pl.ds(hi, CONV_HALO), :] for s in src_refs])
            prev = jnp.where(i > 0, prev, jnp.zeros_like(prev))
            nxt = jnp.where(i < n_blk - 1, nxt, jnp.zeros_like(nxt))
            window = jnp.concatenate([prev, cur, nxt], axis=0)
            shifted = jnp.dot(shift, window, preferred_element_type=F32)
            w = lanes([r[...] for r in w_refs])
            acc = lanes([r[...] for r in b_refs]) + w[pad:pad + 1, :] * cur.astype(F32)
            for k, j in enumerate(taps):
                acc = acc + w[j:j + 1, :] * shifted[k * CHUNK:(k + 1) * CHUNK, :]
            store(r0, _silu(acc))

    def body(i, carry):
        for u in range(unroll):
            block(i * unroll + u)
        return carry

    lax.fori_loop(0, n_blk // unroll, body, 0)


def _ssd_kernel(x_lat, b_lat, c_lat, x_ctx, b_ctx, c_ctx,
                cwx, cwb, cwc, cbx, cbb, cbc,
                dt_lat, dt_ctx, aneg_ref, dskip_ref, z_ref, ng_ref,
                o_ref,
                xc, bt, cc, y_scr, dt_scr, u_scr, xt_scr, st_f, st_b, *, seq, n_ctx):
    nh = HEADS_PER_GROUP
    n_ctx_chunks = n_ctx // CHUNK
    n_lat_chunks = seq // CHUNK
    n_chunks = n_ctx_chunks + n_lat_chunks

    def store_x(off):
        def f(r0, v):
            xc[pl.ds(pl.multiple_of(off + r0, CHUNK), CHUNK), :] = v
        return f

    def store_bc(off):
        def f(r0, v):
            r = pl.multiple_of(off + r0, CHUNK)
            bt[:, pl.ds(r, CHUNK)] = v[:, :D_STATE].T.astype(BF16)
            cc[pl.ds(r, CHUNK), :] = v[:, D_STATE:].astype(BF16)
        return f

    shift, taps = _conv_shift_matrix()
    _conv_silu([([x_ctx], [cwx], [cbx], store_x(0)),
                ([b_ctx, c_ctx], [cwb, cwc], [cbb, cbc], store_bc(0))], n_ctx, shift, taps, CONV_UNROLL)
    _conv_silu([([x_lat], [cwx], [cbx], store_x(n_ctx)),
                ([b_lat, c_lat], [cwb, cwc], [cbb, cbc], store_bc(n_ctx))], seq, shift, taps, CONV_UNROLL)

    row = lax.broadcasted_iota(jnp.int32, (CHUNK, CHUNK), 0)
    col = lax.broadcasted_iota(jnp.int32, (CHUNK, CHUNK), 1)
    triu = (row <= col).astype(BF16)
    eye = (row == col).astype(BF16)
    lower = row >= col
    upper = row <= col
    dskip = dskip_ref[...]
    lane_head = lax.broadcasted_iota(jnp.int32, (CHUNK, GROUP_CH), 1) // SSM_HEAD_DIM

    def spread(n_cols, q_of_col, hd_of_col):
        k = lax.broadcasted_iota(jnp.int32, (CHUNK, n_cols), 0)
        c = lax.broadcasted_iota(jnp.int32, (CHUNK, n_cols), 1)
        return ((k < 72) & (k // 24 == q_of_col(c)) & (k % 8 == hd_of_col(c))).astype(BF16)

    n_cb = 2 * nh * CHUNK
    rmat_f = spread(n_cb + 2 * GROUP_CH,
                    lambda c: jnp.where(c < n_cb, 0, jnp.where(c < n_cb + GROUP_CH, 1, 2)),
                    lambda c: jnp.where(c < n_cb, c // CHUNK, ((c - n_cb) % GROUP_CH) // SSM_HEAD_DIM))
    rmat_b = spread(2 * GROUP_CH,
                    lambda c: jnp.where(c < GROUP_CH, 1, 2),
                    lambda c: nh + (c % GROUP_CH) // SSM_HEAD_DIM)

    dt_scr[0:n_ctx_chunks] = dt_ctx[...]
    dt_scr[n_ctx_chunks:n_chunks] = dt_lat[...]
    dt3 = dt_scr[...]
    n_rows = n_chunks * 8
    dt2 = dt3.reshape(n_rows, CHUNK)
    a2 = (dt3 * aneg_ref[...][None]).reshape(n_rows, CHUNK)
    cum = sum(jnp.dot(part.astype(BF16), triu, preferred_element_type=F32) for part in _split3(a2))
    total = jnp.broadcast_to(cum[:, CHUNK - 1:CHUNK], cum.shape)
    cumx = cum - a2
    is_fwd = (lax.broadcasted_iota(jnp.int32, (n_rows, CHUNK), 0) % 8) < nh
    u2 = jnp.where(is_fwd, cum, cumx)
    w2 = dt2 * jnp.exp(jnp.where(is_fwd, total - cum, cumx))
    e2 = jnp.exp(jnp.where(is_fwd, cum, total - cumx))
    u_scr[...] = u2.reshape(n_chunks, 8, CHUNK)
    pieces = [p.reshape(n_chunks, 8, CHUNK) for v in (u2, w2, e2) for p in _split3(v)]
    pieces.append(jnp.zeros((n_chunks, CHUNK - 8 * len(pieces), CHUNK), F32))
    packed = jnp.concatenate(pieces, axis=1).reshape(n_chunks * CHUNK, CHUNK).astype(BF16)
    xt_scr[...] = lax.dot_general(eye, packed, (((1,), (1,)), ((), ())),
                                  preferred_element_type=F32).astype(BF16)

    def fwd_chunk(ci, y_off_row, st_prev):
        r0 = pl.multiple_of(ci * CHUNK, CHUNK)
        xt = xt_scr[:, pl.ds(r0, CHUNK)]
        xs = xc[pl.ds(r0, CHUNK), :]
        btc = bt[:, pl.ds(r0, CHUNK)]
        if y_off_row is None:
            big = jnp.dot(xt, rmat_f[:, n_cb:], preferred_element_type=F32)
            w_col, e_col = big[:, :GROUP_CH], big[:, GROUP_CH:]
        else:
            big = jnp.dot(xt, rmat_f, preferred_element_type=F32)
            cb, w_col, e_col = big[:, :n_cb], big[:, n_cb:n_cb + GROUP_CH], big[:, n_cb + GROUP_CH:]
            u8 = u_scr[ci]
            dt8 = dt_scr[ci]
            cm = cc[pl.ds(r0, CHUNK), :]
            g = jnp.dot(cm, btc, preferred_element_type=F32)
            xs_b = xs.astype(BF16)
            w_parts, x_parts = [], []
            for r in range(nh):
                ef = cb[:, r * CHUNK:(r + 1) * CHUNK] - u8[r:r + 1, :]
                eb = u8[nh + r:nh + r + 1, :] - cb[:, (nh + r) * CHUNK:(nh + r + 1) * CHUNK]
                df = jnp.exp(jnp.where(lower, ef, NEG_BIG)) * dt8[r:r + 1, :]
                db = jnp.exp(jnp.where(upper, eb, NEG_BIG)) * dt8[nh + r:nh + r + 1, :]
                w_parts.append((g * (df + db)).astype(BF16))
                x_parts.append(jnp.where(lane_head == r, xs_b, jnp.zeros_like(xs_b)))
            y = jnp.dot(jnp.concatenate(w_parts, axis=1), jnp.concatenate(x_parts, axis=0),
                        preferred_element_type=F32)
            y = y + e_col * jnp.dot(cm, st_prev.astype(BF16), preferred_element_type=F32)
            y_scr[pl.ds(y_off_row, CHUNK), :] = y + dskip * xs
        xw = (xs * w_col).astype(BF16)
        return st_prev * e_col[CHUNK - 1:CHUNK, :] + jnp.dot(btc, xw, preferred_element_type=F32)

    def bwd_chunk(ci, y_off_row, st_prev):
        r0 = pl.multiple_of(ci * CHUNK, CHUNK)
        xt = xt_scr[:, pl.ds(r0, CHUNK)]
        xs = xc[pl.ds(r0, CHUNK), :]
        btc = bt[:, pl.ds(r0, CHUNK)]
        big = jnp.dot(xt, rmat_b, preferred_element_type=F32)
        w_col, e_col = big[:, :GROUP_CH], big[:, GROUP_CH:]
        if y_off_row is not None:
            cm = cc[pl.ds(r0, CHUNK), :]
            y = y_scr[pl.ds(y_off_row, CHUNK), :]
            y = y + e_col * jnp.dot(cm, st_prev.astype(BF16), preferred_element_type=F32)
            gated = y * _silu(z_ref[pl.ds(y_off_row, CHUNK), :].astype(F32))
            ms = jnp.mean(gated * gated, axis=-1, keepdims=True)
            o_ref[pl.ds(y_off_row, CHUNK), :] = ((gated * lax.rsqrt(ms + EPS)) * ng_ref[...]).astype(o_ref.dtype)
        xw = (xs * w_col).astype(BF16)
        return st_prev * e_col[0:1, :] + jnp.dot(btc, xw, preferred_element_type=F32)

    unroll = math.gcd(SSD_UNROLL, n_lat_chunks)

    st = jnp.zeros((D_STATE, GROUP_CH), F32)
    for c in range(n_ctx_chunks):
        st = fwd_chunk(c, None, st)
    st_f[...] = st

    def fwd_body(i, carry):
        s = st_f[...]
        for u in range(unroll):
            c = i * unroll + u
            s = fwd_chunk(n_ctx_chunks + c, pl.multiple_of(c * CHUNK, CHUNK), s)
        st_f[...] = s
        return carry

    lax.fori_loop(0, n_lat_chunks // unroll, fwd_body, 0)

    st = jnp.zeros((D_STATE, GROUP_CH), F32)
    for c in reversed(range(n_ctx_chunks)):
        st = bwd_chunk(c, None, st)
    st_b[...] = st

    def bwd_body(i, carry):
        s = st_b[...]
        for u in range(unroll):
            c = n_lat_chunks - 1 - (i * unroll + u)
            s = bwd_chunk(n_ctx_chunks + c, pl.multiple_of(c * CHUNK, CHUNK), s)
        st_b[...] = s
        return carry

    lax.fori_loop(0, n_lat_chunks // unroll, bwd_body, 0)


def _ssd(xbc_all, conv_w, conv_b, dt_c, aneg_col, dskip_row, z, norm_g, batch, seq, n_ctx):
    d_ssm = SSM_HEADS * SSM_HEAD_DIM
    b0 = d_ssm // D_STATE
    c0 = b0 + SSM_GROUPS
    ctx_blk0 = batch * seq // n_ctx
    n_ctx_chunks, n_lat_chunks = n_ctx // CHUNK, seq // CHUNK
    n_tok = n_ctx + seq
    in_specs = [
        pl.BlockSpec((seq, GROUP_CH), lambda b, g: (b, g)),
        pl.BlockSpec((seq, D_STATE), lambda b, g: (b, b0 + g)),
        pl.BlockSpec((seq, D_STATE), lambda b, g: (b, c0 + g)),
        pl.BlockSpec((n_ctx, GROUP_CH), lambda b, g: (ctx_blk0 + b, g)),
        pl.BlockSpec((n_ctx, D_STATE), lambda b, g: (ctx_blk0 + b, b0 + g)),
        pl.BlockSpec((n_ctx, D_STATE), lambda b, g: (ctx_blk0 + b, c0 + g)),
        pl.BlockSpec((CONV_W, GROUP_CH), lambda b, g: (0, g)),
        pl.BlockSpec((CONV_W, D_STATE), lambda b, g: (0, b0 + g)),
        pl.BlockSpec((CONV_W, D_STATE), lambda b, g: (0, c0 + g)),
        pl.BlockSpec((1, GROUP_CH), lambda b, g: (0, g)),
        pl.BlockSpec((1, D_STATE), lambda b, g: (0, b0 + g)),
        pl.BlockSpec((1, D_STATE), lambda b, g: (0, c0 + g)),
        pl.BlockSpec((n_lat_chunks, 8, CHUNK), lambda b, g: (b, g, 0)),
        pl.BlockSpec((n_ctx_chunks, 8, CHUNK), lambda b, g: (ctx_blk0 + b, g, 0)),
        pl.BlockSpec((8, 1), lambda b, g: (g, 0)),
        pl.BlockSpec((1, GROUP_CH), lambda b, g: (0, g)),
        pl.BlockSpec((seq, GROUP_CH), lambda b, g: (b, g)),
        pl.BlockSpec((1, GROUP_CH), lambda b, g: (0, g)),
    ]
    return pl.pallas_call(
        functools.partial(_ssd_kernel, seq=seq, n_ctx=n_ctx),
        grid=(batch, SSM_GROUPS),
        in_specs=in_specs,
        out_specs=pl.BlockSpec((seq, GROUP_CH), lambda b, g: (b, g)),
        out_shape=jax.ShapeDtypeStruct((batch * seq, d_ssm), BF16),
        scratch_shapes=[pltpu.VMEM((n_tok, GROUP_CH), F32),
                        pltpu.VMEM((D_STATE, n_tok), BF16),
                        pltpu.VMEM((n_tok, D_STATE), BF16),
                        pltpu.VMEM((seq, GROUP_CH), F32),
                        pltpu.VMEM((n_tok // CHUNK, 8, CHUNK), F32),
                        pltpu.VMEM((n_tok // CHUNK, 8, CHUNK), F32),
                        pltpu.VMEM((CHUNK, n_tok), BF16),
                        pltpu.VMEM((D_STATE, GROUP_CH), F32),
                        pltpu.VMEM((D_STATE, GROUP_CH), F32)],
        compiler_params=_cparams(2),
        name="ssd",
    )(xbc_all, xbc_all, xbc_all, xbc_all, xbc_all, xbc_all,
      conv_w, conv_w, conv_w, conv_b, conv_b, conv_b,
      dt_c, dt_c, aneg_col, dskip_row, z, norm_g)


def _rope_tables(seq):
    n_freq = HEAD_DIM // 4
    pos = np.arange(seq)
    inv = np.power(ROPE_THETA, -np.arange(n_freq, dtype=np.float64) / n_freq)
    ang_r = (pos // GRID_W)[:, None] * inv
    ang_c = (pos % GRID_W)[:, None] * inv
    cos_t = np.concatenate([np.cos(ang_r), np.cos(ang_r), np.cos(ang_c), np.cos(ang_c)], axis=1)
    sin_t = np.concatenate([-np.sin(ang_r), np.sin(ang_r), -np.sin(ang_c), np.sin(ang_c)], axis=1)
    return jnp.asarray(cos_t, F32), jnp.asarray(sin_t, F32)


def kernel(x, c, ctx, c_ctx, w_mod, b_mod, norm1, w_in, conv_w, conv_b, dt_bias, a_log, d_skip,
           ssm_norm, q_norm, k_norm, w_ssm_br, w_attn_br, w_o, norm2, w_ffn_in, w_ffn_out):
    batch, seq, d = x.shape
    n_ctx = ctx.shape[1]
    n_lat = batch * seq
    n_ctx_rows = batch * n_ctx
    n_all = n_lat + n_ctx_rows
    d_ssm = SSM_HEADS * SSM_HEAD_DIM
    d_conv = d_ssm + 2 * SSM_GROUPS * D_STATE
    d_attn = ATTN_HEADS * HEAD_DIM
    d_kv = KV_HEADS * HEAD_DIM
    d_ff = w_ffn_out.shape[1]
    n_dt = 2 * SSM_HEADS
    o_xbc = d_ssm
    o_dt = o_xbc + d_conv
    o_q = o_dt + n_dt
    o_k = o_q + d_attn
    o_g = o_k + 2 * d_kv
    assert w_mod.shape[0] == 1, "single-layer block"

    cvec = jnp.concatenate([c, c_ctx[None, :], jnp.zeros((8 - batch - 1, d), F32)], axis=0)
    mod = _modulation(cvec, w_mod[0], b_mod[0][None, :])
    mod3 = mod[:batch + 1].reshape(batch + 1, 1, N_MOD * d)

    x2d = x.reshape(n_lat, d)
    h_all = _prenorm1(x2d, ctx.reshape(n_ctx_rows, d), norm1, mod3, batch, 0, 1)

    w_in_t = jnp.transpose(w_in[0])
    tn = 1024
    tm_lat = 1024 if seq % 1024 == 0 else 512
    tm_all = n_all // 8 if (n_all // 8) % 16 == 0 and n_all % 8 == 0 else 512

    def in_proj(row_off, width, rows, tm_rows, out_dtype, name):
        return _matmul([(h_all, 0)], [(w_in_t, row_off, 0)], [], _ep_plain, width, out_dtype,
                       m_rows=rows, tm=tm_rows, tn=tn, name=name, w_t=True)

    z = in_proj(0, d_ssm, n_lat, tm_lat, BF16, "proj_z")
    xbc_all = in_proj(o_xbc, d_conv, n_all, tm_all, BF16, "proj_xbc")
    q_raw = in_proj(o_q, d_attn, n_lat, tm_lat, BF16, "proj_q")
    kv_all = in_proj(o_k, 2 * d_kv, n_all, tm_all, BF16, "proj_kv")
    gates = in_proj(o_g, 2 * d, n_lat, tm_lat, BF16, "proj_gates")

    perm = np.array([dr * SSM_HEADS + g * HEADS_PER_GROUP + r
                     for g in range(SSM_GROUPS) for dr in range(2) for r in range(HEADS_PER_GROUP)])
    w_dt_t = w_in_t[o_dt:o_dt + n_dt][perm].astype(BF16)
    dt_bias_col = dt_bias[0].reshape(n_dt)[perm][:, None]
    aneg_col = (-jnp.exp(a_log[0].astype(F32))).reshape(n_dt)[perm][:, None]
    dt_c = _dt_proj(w_dt_t, h_all, dt_bias_col)

    dskip_row = jnp.repeat(d_skip[0].astype(F32), SSM_HEAD_DIM)[None, :]
    y_norm = _ssd(xbc_all, conv_w[0], conv_b[0][None, :], dt_c, aneg_col, dskip_row, z,
                  ssm_norm, batch, seq, n_ctx)

    cos_t, sin_t = _rope_tables(seq)
    attn = _attention(q_raw, kv_all, q_norm, k_norm, cos_t, sin_t, batch, seq, n_ctx)

    tn_m = 512
    merged = _matmul([(y_norm, 0), (attn, 0)], [(w_ssm_br[0], 0, 0), (w_attn_br[0], 0, 1)],
                     [(gates, (tm_lat, tn_m), lambda j, m: (m, j)),
                      (gates, (tm_lat, tn_m), lambda j, m: (m, j + d // tn_m))],
                     _ep_merge, d, BF16, m_rows=n_lat, tm=tm_lat, tn=tn_m, name="branch_merge")
    tiles_per_batch = seq // tm_lat
    x_mid = _matmul([(merged, 0)], [(w_o[0], 0, 0)],
                    [(x2d, (tm_lat, tn), lambda j, m: (m, j)),
                     (mod3, (1, 1, tn), lambda j, m: (m // tiles_per_batch, 0, 2 * (d // tn) + j))],
                    _ep_gated_residual, d, F32, m_rows=n_lat, tm=tm_lat, tn=tn, name="out_proj")

    h2 = _prenorm2(x_mid, norm2, mod3, batch, 3, 4)
    tn_f = 512
    act = _matmul([(h2, 0)], [(w_ffn_in[0], 0, 0), (w_ffn_in[0], d_ff // tn_f, 0)], [], _ep_swiglu,
                  d_ff, BF16, m_rows=n_lat, tm=tm_lat, tn=tn_f, name="ffn_in")
    tm_o = 512
    tiles_per_batch_o = seq // tm_o
    out = _matmul([(act, 0)], [(w_ffn_out[0], 0, 0)],
                  [(x_mid, (tm_o, tn_f), lambda j, m: (m, j)),
                   (mod3, (1, 1, tn_f), lambda j, m: (m // tiles_per_batch_o, 0, 5 * (d // tn_f) + j))],
                  _ep_gated_residual, d, F32, m_rows=n_lat, tm=tm_o, tn=tn_f, name="ffn_out")
    return out.reshape(batch, seq, d)
```

```python
import functools
import math

import jax
import jax.numpy as jnp
import numpy as np
from jax import lax
from jax.experimental import pallas as pl
from jax.experimental.pallas import tpu as pltpu

F32 = jnp.float32
BF16 = jnp.bfloat16

GRID_W = 64
SSM_HEADS = 32
SSM_HEAD_DIM = 64
SSM_GROUPS = 8
D_STATE = 128
CONV_W = 5
CHUNK = 128
ATTN_HEADS = 16
KV_HEADS = 4
HEAD_DIM = 128
ROPE_THETA = 10000.0
N_MOD = 6
EPS = 1e-6

HEADS_PER_GROUP = SSM_HEADS // SSM_GROUPS
GROUP_CH = HEADS_PER_GROUP * SSM_HEAD_DIM
Q_REP = ATTN_HEADS // KV_HEADS

VMEM_LIMIT_BYTES = 56 * 1024 * 1024
NEG_BIG = -1e30
LOG2_E = 1.4426950408889634
SSD_UNROLL = 16


def _cparams(n_grid):
    return pltpu.CompilerParams(dimension_semantics=("arbitrary",) * n_grid,
                                vmem_limit_bytes=VMEM_LIMIT_BYTES)


def _silu(v):
    return v * jax.nn.sigmoid(v)


def _mod_kernel(c_ref, w_ref, b_ref, o_ref):
    s = _silu(c_ref[...]).astype(BF16)
    o_ref[...] = jnp.dot(s, w_ref[...].astype(BF16), preferred_element_type=F32) + b_ref[...]


def _modulation(cvec, w_mod, b_mod):
    rows, d = cvec.shape
    n = w_mod.shape[1]
    tn = 1024
    return pl.pallas_call(
        _mod_kernel,
        grid=(n // tn,),
        in_specs=[pl.BlockSpec((rows, d), lambda j: (0, 0)),
                  pl.BlockSpec((d, tn), lambda j: (0, j)),
                  pl.BlockSpec((1, tn), lambda j: (0, j))],
        out_specs=pl.BlockSpec((rows, tn), lambda j: (0, j)),
        out_shape=jax.ShapeDtypeStruct((rows, n), F32),
        compiler_params=_cparams(1),
        name="modulation",
    )(cvec, w_mod, b_mod)


PRENORM_ROWS = 512


NORM_ROWS = 16


def _norm_mod(t, g, sh, sc):
    ms = jnp.mean(t * t, axis=-1, keepdims=True)
    return ((t * lax.rsqrt(ms + EPS)) * g) * (1.0 + sc) + sh


def _norm_mod_rows(src_ref, dst_ref, g_ref, sh_ref, sc_ref):
    g = g_ref[...]
    sh = sh_ref[0]
    sc = sc_ref[0]

    def body(k, carry):
        rows = pl.ds(pl.multiple_of(k * NORM_ROWS, NORM_ROWS), NORM_ROWS)
        dst_ref[rows, :] = _norm_mod(src_ref[rows, :], g, sh, sc).astype(dst_ref.dtype)
        return carry

    lax.fori_loop(0, src_ref.shape[0] // NORM_ROWS, body, 0, unroll=8)


def _prenorm1_kernel(x_ref, ctx_ref, g_ref, sh_ref, sc_ref, wdt_ref, bdt_ref, o_ref, dt_ref, *, n_lat_tiles):
    i = pl.program_id(0)

    @pl.when(i < n_lat_tiles)
    def _():
        _norm_mod_rows(x_ref, o_ref, g_ref, sh_ref, sc_ref)

    @pl.when(i >= n_lat_tiles)
    def _():
        _norm_mod_rows(ctx_ref, o_ref, g_ref, sh_ref, sc_ref)

    raw = lax.dot_general(wdt_ref[...], o_ref[...], (((1,), (1,)), ((), ())),
                          preferred_element_type=F32) + bdt_ref[...]
    dt = jnp.maximum(raw, 0.0) + jnp.log1p(jnp.exp(-jnp.abs(raw)))
    for c in range(dt_ref.shape[0]):
        dt_ref[c] = dt[:, c * CHUNK:(c + 1) * CHUNK]


def _prenorm1(x2d, ctx2d, norm_g, mod3, batch, shift_blk, scale_blk, w_dt_t, dt_bias_col):
    n_lat, d = x2d.shape
    n_ctx = ctx2d.shape[0]
    n_dt = w_dt_t.shape[0]
    tr = math.gcd(PRENORM_ROWS, math.gcd(n_lat // batch, n_ctx))
    lat_tiles, ctx_tiles = n_lat // tr, n_ctx // tr
    tiles_per_batch = lat_tiles // batch

    def mod_row(i):
        return jnp.where(i < lat_tiles, i // tiles_per_batch, batch)

    return pl.pallas_call(
        functools.partial(_prenorm1_kernel, n_lat_tiles=lat_tiles),
        grid=(lat_tiles + ctx_tiles,),
        in_specs=[pl.BlockSpec((tr, d), lambda i: (jnp.minimum(i, lat_tiles - 1), 0)),
                  pl.BlockSpec((tr, d), lambda i: (jnp.maximum(i - lat_tiles, 0), 0)),
                  pl.BlockSpec((1, d), lambda i: (0, 0)),
                  pl.BlockSpec((1, 1, d), lambda i: (mod_row(i), 0, shift_blk)),
                  pl.BlockSpec((1, 1, d), lambda i: (mod_row(i), 0, scale_blk)),
                  pl.BlockSpec((n_dt, d), lambda i: (0, 0)),
                  pl.BlockSpec((n_dt, 1), lambda i: (0, 0))],
        out_specs=[pl.BlockSpec((tr, d), lambda i: (i, 0)),
                   pl.BlockSpec((tr // CHUNK, n_dt, CHUNK), lambda i: (i, 0, 0))],
        out_shape=[jax.ShapeDtypeStruct((n_lat + n_ctx, d), BF16),
                   jax.ShapeDtypeStruct(((n_lat + n_ctx) // CHUNK, n_dt, CHUNK), F32)],
        compiler_params=_cparams(1),
        name="prenorm1",
    )(x2d, ctx2d, norm_g, mod3, mod3, w_dt_t, dt_bias_col)


OUT_PROJ_ROWS = 512


def _out_proj_kernel(a_ref, w_ref, x_ref, gt_ref, g_ref, sh_ref, sc_ref, xmid_ref, h_ref, w_scr):
    @pl.when(pl.program_id(0) == 0)
    def _():
        w_scr[...] = w_ref[...].astype(BF16)

    xmid_ref[...] = x_ref[...] + gt_ref[0] * jnp.dot(a_ref[...], w_scr[...], preferred_element_type=F32)
    _norm_mod_rows(xmid_ref, h_ref, g_ref, sh_ref, sc_ref)


def _out_proj(a, w, x2d, norm_g, mod3, batch, gate_blk, shift_blk, scale_blk):
    n_lat, d = x2d.shape
    tr = math.gcd(OUT_PROJ_ROWS, n_lat // batch)
    tiles_per_batch = n_lat // tr // batch

    def mod_spec(blk):
        return pl.BlockSpec((1, 1, d), lambda i: (i // tiles_per_batch, 0, blk))

    return pl.pallas_call(
        _out_proj_kernel,
        grid=(n_lat // tr,),
        in_specs=[pl.BlockSpec((tr, a.shape[1]), lambda i: (i, 0)),
                  pl.BlockSpec(w.shape, lambda i: (0, 0), pipeline_mode=pl.Buffered(1)),
                  pl.BlockSpec((tr, d), lambda i: (i, 0)),
                  mod_spec(gate_blk),
                  pl.BlockSpec((1, d), lambda i: (0, 0)),
                  mod_spec(shift_blk),
                  mod_spec(scale_blk)],
        out_specs=[pl.BlockSpec((tr, d), lambda i: (i, 0)),
                   pl.BlockSpec((tr, d), lambda i: (i, 0))],
        out_shape=[jax.ShapeDtypeStruct((n_lat, d), F32),
                   jax.ShapeDtypeStruct((n_lat, d), BF16)],
        scratch_shapes=[pltpu.VMEM(w.shape, BF16)],
        compiler_params=_cparams(1),
        name="out_proj",
    )(a, w, x2d, mod3, norm_g, mod3, mod3)


def _mm_kernel(*refs, n_a, which_a, n_extra, epilogue, w_t):
    n_w = len(which_a)
    a_refs = refs[:n_a]
    w_refs = refs[n_a:n_a + n_w]
    e_refs = refs[n_a + n_w:n_a + n_w + n_extra]
    o_ref = refs[n_a + n_w + n_extra]
    w_scr = refs[n_a + n_w + n_extra + 1:]

    @pl.when(pl.program_id(1) == 0)
    def _():
        for k in range(n_w):
            w_scr[k][...] = w_refs[k][...].astype(BF16)

    dims = (((1,), (1,)), ((), ())) if w_t else (((1,), (0,)), ((), ()))
    accs = [lax.dot_general(a_refs[which_a[k]][...], w_scr[k][...], dims, preferred_element_type=F32)
            for k in range(n_w)]
    o_ref[...] = epilogue(accs, e_refs).astype(o_ref.dtype)


def _matmul(a_list, w_list, extras, epilogue, n_out, out_dtype, *, m_rows, tm, tn, name, w_t=False):
    grid = (n_out // tn, m_rows // tm)
    in_specs, args, scratch = [], [], []
    for a, off in a_list:
        in_specs.append(pl.BlockSpec((tm, a.shape[1]), lambda j, m, off=off: (m + off, 0)))
        args.append(a)
    for w, off, _ in w_list:
        if w_t:
            in_specs.append(pl.BlockSpec((pl.Element(tn), pl.Element(w.shape[1])),
                                         lambda j, m, off=off: (pl.multiple_of(off(j), 8), 0)))
            scratch.append(pltpu.VMEM((tn, w.shape[1]), BF16))
        else:
            in_specs.append(pl.BlockSpec((w.shape[0], tn), lambda j, m, off=off: (0, j + off)))
            scratch.append(pltpu.VMEM((w.shape[0], tn), BF16))
        args.append(w)
    for e, blk, imap in extras:
        in_specs.append(pl.BlockSpec(blk, imap))
        args.append(e)
    kern = functools.partial(_mm_kernel, n_a=len(a_list), which_a=tuple(w[2] for w in w_list),
                             n_extra=len(extras), epilogue=epilogue, w_t=w_t)
    return pl.pallas_call(
        kern,
        grid=grid,
        in_specs=in_specs,
        out_specs=pl.BlockSpec((tm, tn), lambda j, m: (m, j)),
        out_shape=jax.ShapeDtypeStruct((m_rows, n_out), out_dtype),
        scratch_shapes=scratch,
        compiler_params=_cparams(2),
        name=name,
    )(*args)


def _ep_plain(accs, e_refs):
    return accs[0]


def _ep_swiglu(accs, e_refs):
    return _silu(accs[0]) * accs[1]


def _ep_merge(accs, e_refs):
    g_ssm = e_refs[0][...].astype(F32)
    g_attn = e_refs[1][...].astype(F32)
    return jax.nn.sigmoid(g_ssm) * accs[0] + jax.nn.sigmoid(g_attn) * accs[1]


def _ep_gated_residual(accs, e_refs):
    return e_refs[0][...] + e_refs[1][0] * accs[0]


def _norm_rope(t, g, cs, sn, scale):
    tn = (t * lax.rsqrt(jnp.mean(t * t, axis=-1, keepdims=True) + EPS)) * g
    lane = lax.broadcasted_iota(jnp.int32, tn.shape, 1)
    first_half = (lane % (HEAD_DIM // 2)) < (HEAD_DIM // 4)
    rot = jnp.where(first_half,
                    pltpu.roll(tn, HEAD_DIM - HEAD_DIM // 4, 1),
                    pltpu.roll(tn, HEAD_DIM // 4, 1))
    return (tn * cs + rot * sn) * scale


def _attn_kernel(q_ref, kl_ref, kc_ref, vl_ref, vc_ref, qg_ref, kg_ref, cos_ref, sin_ref, o_ref,
                 k_scr, v_scr, s0_scr, s1_scr, p0_scr, p1_scr, *, n_ctx, tq, q_scale):
    seq = q_ref.shape[0]
    n_tiles = seq // tq
    kg = kg_ref[...]
    qg = qg_ref[...]

    kc = kc_ref[...].astype(F32)
    k_scr[0:n_ctx, :] = ((kc * lax.rsqrt(jnp.mean(kc * kc, axis=-1, keepdims=True) + EPS)) * kg).astype(BF16)

    def k_body(i, carry):
        r0 = pl.multiple_of(i * tq, tq)
        kt = _norm_rope(kl_ref[pl.ds(r0, tq), :].astype(F32), kg,
                        cos_ref[pl.ds(r0, tq), :], sin_ref[pl.ds(r0, tq), :], 1.0)
        k_scr[pl.ds(pl.multiple_of(n_ctx + r0, tq), tq), :] = kt.astype(BF16)
        return carry

    lax.fori_loop(0, n_tiles, k_body, 0)
    v_scr[0:n_ctx, 0:HEAD_DIM] = vc_ref[...]
    v_scr[n_ctx:, 0:HEAD_DIM] = vl_ref[...]
    v_scr[:, HEAD_DIM:] = jnp.ones((v_scr.shape[0], HEAD_DIM), BF16)

    s_bufs = (s0_scr, s1_scr)
    p_bufs = (p0_scr, p1_scr)

    def scores(t, r):
        r0 = pl.multiple_of(t * tq, tq)
        q = _norm_rope(q_ref[pl.ds(r0, tq), r * HEAD_DIM:(r + 1) * HEAD_DIM].astype(F32), qg,
                       cos_ref[pl.ds(r0, tq), :], sin_ref[pl.ds(r0, tq), :], q_scale).astype(BF16)
        s_bufs[r % 2][...] = lax.dot_general(q, k_scr[...], (((1,), (1,)), ((), ())),
                                             preferred_element_type=F32)

    def softmax(r):
        sc = s_bufs[r % 2][...]
        mx = jnp.max(sc, axis=-1, keepdims=True)
        p_bufs[r % 2][...] = jnp.exp2((sc - mx).astype(BF16))

    def values(t, r):
        r0 = pl.multiple_of(t * tq, tq)
        o = jnp.dot(p_bufs[r % 2][...], v_scr[...], preferred_element_type=F32)
        den = o[:, HEAD_DIM:HEAD_DIM + 1]
        o_ref[pl.ds(r0, tq), r * HEAD_DIM:(r + 1) * HEAD_DIM] = (o[:, :HEAD_DIM] / den).astype(o_ref.dtype)

    def tile(t, first):
        for r in range(Q_REP):
            if r >= 1:
                softmax(r - 1)
            elif not first:
                softmax(Q_REP - 1)
            if r >= 2:
                values(t, r - 2)
            elif not first:
                values(t - 1, Q_REP - 2 + r)
            scores(t, r)

    tile(0, True)

    def body(t, carry):
        tile(t, False)
        return carry

    lax.fori_loop(1, n_tiles, body, 0)
    softmax(Q_REP - 1)
    values(n_tiles - 1, Q_REP - 2)
    values(n_tiles - 1, Q_REP - 1)


def _attention(p_lat, c_q, p_all, c_k, c_v, q_norm, k_norm, cos_t, sin_t, batch, seq, n_ctx):
    tq = 256
    assert Q_REP % 2 == 0 and Q_REP >= 2
    ctx_blk0 = batch * seq // n_ctx
    gw = Q_REP * HEAD_DIM
    n_keys = n_ctx + seq
    qb, kb, vb = c_q // gw, c_k // HEAD_DIM, c_v // HEAD_DIM
    const = dict(pipeline_mode=pl.Buffered(1))
    return pl.pallas_call(
        functools.partial(_attn_kernel, n_ctx=n_ctx, tq=tq, q_scale=HEAD_DIM ** -0.5 * LOG2_E),
        grid=(batch, KV_HEADS),
        in_specs=[pl.BlockSpec((seq, gw), lambda b, g: (b, qb + g)),
                  pl.BlockSpec((seq, HEAD_DIM), lambda b, g: (b, kb + g)),
                  pl.BlockSpec((n_ctx, HEAD_DIM), lambda b, g: (ctx_blk0 + b, kb + g)),
                  pl.BlockSpec((seq, HEAD_DIM), lambda b, g: (b, vb + g)),
                  pl.BlockSpec((n_ctx, HEAD_DIM), lambda b, g: (ctx_blk0 + b, vb + g)),
                  pl.BlockSpec((1, HEAD_DIM), lambda b, g: (0, 0)),
                  pl.BlockSpec((1, HEAD_DIM), lambda b, g: (0, 0)),
                  pl.BlockSpec((seq, HEAD_DIM), lambda b, g: (0, 0), **const),
                  pl.BlockSpec((seq, HEAD_DIM), lambda b, g: (0, 0), **const)],
        out_specs=pl.BlockSpec((seq, gw), lambda b, g: (b, g)),
        out_shape=jax.ShapeDtypeStruct((batch * seq, ATTN_HEADS * HEAD_DIM), BF16),
        scratch_shapes=[pltpu.VMEM((n_keys, HEAD_DIM), BF16),
                        pltpu.VMEM((n_keys, 2 * HEAD_DIM), BF16),
                        pltpu.VMEM((tq, n_keys), F32),
                        pltpu.VMEM((tq, n_keys), F32),
                        pltpu.VMEM((tq, n_keys), BF16),
                        pltpu.VMEM((tq, n_keys), BF16)],
        compiler_params=_cparams(2),
        name="attention",
    )(p_lat, p_all, p_all, p_all, p_all, q_norm, k_norm, cos_t, sin_t)


def _split3(v):
    hi = v.astype(BF16).astype(F32)
    r1 = v - hi
    mid = r1.astype(BF16).astype(F32)
    lo = (r1 - mid).astype(BF16).astype(F32)
    return hi, mid, lo


CONV_HALO = 16
CONV_UNROLL = 8


def _conv_shift_matrix():
    pad = (CONV_W - 1) // 2
    taps = [j for j in range(CONV_W) if j != pad]
    rows = lax.broadcasted_iota(jnp.int32, (len(taps) * CHUNK, CHUNK + 2 * CONV_HALO), 0)
    cols = lax.broadcasted_iota(jnp.int32, (len(taps) * CHUNK, CHUNK + 2 * CONV_HALO), 1)
    tap = jnp.where(rows // CHUNK < pad, rows // CHUNK, rows // CHUNK + 1)
    return (cols == CONV_HALO - pad + tap + rows % CHUNK).astype(BF16), taps


def _conv_silu(streams, n_rows, shift, taps, unroll):
    n_blk = n_rows // CHUNK
    pad = (CONV_W - 1) // 2
    unroll = math.gcd(unroll, n_blk)

    def lanes(parts):
        return parts[0] if len(parts) == 1 else jnp.concatenate(parts, axis=1)

    def block(i):
        r0 = pl.multiple_of(i * CHUNK, CHUNK)
        lo = pl.multiple_of(jnp.maximum(r0 - CONV_HALO, 0), CONV_HALO)
        hi = pl.multiple_of(jnp.minimum(r0 + CHUNK, n_rows - CONV_HALO), CONV_HALO)
        for src_refs, w_refs, b_refs, store in streams:
            cur = lanes([s[pl.ds(r0, CHUNK), :] for s in src_refs])
            prev = lanes([s[pl.ds(lo, CONV_HALO), :] for s in src_refs])
            nxt = lanes([s[pl.ds(hi, CONV_HALO), :] for s in src_refs])
            prev = jnp.where(i > 0, prev, jnp.zeros_like(prev))
            nxt = jnp.where(i < n_blk - 1, nxt, jnp.zeros_like(nxt))
            window = jnp.concatenate([prev, cur, nxt], axis=0)
            shifted = jnp.dot(shift, window, preferred_element_type=F32)
            w = lanes([r[...] for r in w_refs])
            acc = lanes([r[...] for r in b_refs]) + w[pad:pad + 1, :] * cur.astype(F32)
            for k, j in enumerate(taps):
                acc = acc + w[j:j + 1, :] * shifted[k * CHUNK:(k + 1) * CHUNK, :]
            store(r0, _silu(acc))

    def body(i, carry):
        for u in range(unroll):
            block(i * unroll + u)
        return carry

    lax.fori_loop(0, n_blk // unroll, body, 0)


def _ssd_kernel(x_lat, b_lat, c_lat, x_ctx, b_ctx, c_ctx,
                cwx, cwb, cwc, cbx, cbb, cbc,
                dt_lat, dt_ctx, aneg_ref, dskip_ref, z_ref, ng_ref,
                o_ref,
                xc, bt, cc, y_scr, dt_scr, u_scr, xt_scr, st_f, st_b, *, seq, n_ctx):
    nh = HEADS_PER_GROUP
    n_ctx_chunks = n_ctx // CHUNK
    n_lat_chunks = seq // CHUNK
    n_chunks = n_ctx_chunks + n_lat_chunks

    def store_x(off):
        def f(r0, v):
            xc[pl.ds(pl.multiple_of(off + r0, CHUNK), CHUNK), :] = v
        return f

    def store_bc(off):
        def f(r0, v):
            r = pl.multiple_of(off + r0, CHUNK)
            bt[:, pl.ds(r, CHUNK)] = v[:, :D_STATE].T.astype(BF16)
            cc[pl.ds(r, CHUNK), :] = v[:, D_STATE:].astype(BF16)
        return f

    shift, taps = _conv_shift_matrix()
    _conv_silu([([x_ctx], [cwx], [cbx], store_x(0)),
                ([b_ctx, c_ctx], [cwb, cwc], [cbb, cbc], store_bc(0))], n_ctx, shift, taps, CONV_UNROLL)
    _conv_silu([([x_lat], [cwx], [cbx], store_x(n_ctx)),
                ([b_lat, c_lat], [cwb, cwc], [cbb, cbc], store_bc(n_ctx))], seq, shift, taps, CONV_UNROLL)

    row = lax.broadcasted_iota(jnp.int32, (CHUNK, CHUNK), 0)
    col = lax.broadcasted_iota(jnp.int32, (CHUNK, CHUNK), 1)
    triu = (row <= col).astype(BF16)
    eye = (row == col).astype(BF16)
    lower = row >= col
    upper = row <= col
    dskip = dskip_ref[...]
    lane_head = lax.broadcasted_iota(jnp.int32, (CHUNK, GROUP_CH), 1) // SSM_HEAD_DIM

    def spread(n_cols, q_of_col, hd_of_col):
        k = lax.broadcasted_iota(jnp.int32, (CHUNK, n_cols), 0)
        c = lax.broadcasted_iota(jnp.int32, (CHUNK, n_cols), 1)
        return ((k < 72) & (k // 24 == q_of_col(c)) & (k % 8 == hd_of_col(c))).astype(BF16)

    n_cb = 2 * nh * CHUNK
    rmat_f = spread(n_cb + 2 * GROUP_CH,
                    lambda c: jnp.where(c < n_cb, 0, jnp.where(c < n_cb + GROUP_CH, 1, 2)),
                    lambda c: jnp.where(c < n_cb, c // CHUNK, ((c - n_cb) % GROUP_CH) // SSM_HEAD_DIM))
    rmat_b = spread(2 * GROUP_CH,
                    lambda c: jnp.where(c < GROUP_CH, 1, 2),
                    lambda c: nh + (c % GROUP_CH) // SSM_HEAD_DIM)

    dt_scr[0:n_ctx_chunks] = dt_ctx[...]
    dt_scr[n_ctx_chunks:n_chunks] = dt_lat[...]
    dt3 = dt_scr[...]
    n_rows = n_chunks * 8
    dt2 = dt3.reshape(n_rows, CHUNK)
    a2 = (dt3 * aneg_ref[...][None]).reshape(n_rows, CHUNK)
    cum = sum(jnp.dot(part.astype(BF16), triu, preferred_element_type=F32) for part in _split3(a2))
    total = jnp.broadcast_to(cum[:, CHUNK - 1:CHUNK], cum.shape)
    cumx = cum - a2
    is_fwd = (lax.broadcasted_iota(jnp.int32, (n_rows, CHUNK), 0) % 8) < nh
    u2 = jnp.where(is_fwd, cum, cumx)
    w2 = dt2 * jnp.exp(jnp.where(is_fwd, total - cum, cumx))
    e2 = jnp.exp(jnp.where(is_fwd, cum, total - cumx))
    u_scr[...] = u2.reshape(n_chunks, 8, CHUNK)
    pieces = [p.reshape(n_chunks, 8, CHUNK) for v in (u2, w2, e2) for p in _split3(v)]
    pieces.append(jnp.zeros((n_chunks, CHUNK - 8 * len(pieces), CHUNK), F32))
    packed = jnp.concatenate(pieces, axis=1).reshape(n_chunks * CHUNK, CHUNK).astype(BF16)
    xt_scr[...] = lax.dot_general(eye, packed, (((1,), (1,)), ((), ())),
                                  preferred_element_type=F32).astype(BF16)

    def fwd_chunk(ci, y_off_row, st_prev):
        r0 = pl.multiple_of(ci * CHUNK, CHUNK)
        xt = xt_scr[:, pl.ds(r0, CHUNK)]
        xs = xc[pl.ds(r0, CHUNK), :]
        btc = bt[:, pl.ds(r0, CHUNK)]
        if y_off_row is None:
            big = jnp.dot(xt, rmat_f[:, n_cb:], preferred_element_type=F32)
            w_col, e_col = big[:, :GROUP_CH], big[:, GROUP_CH:]
        else:
            big = jnp.dot(xt, rmat_f, preferred_element_type=F32)
            cb, w_col, e_col = big[:, :n_cb], big[:, n_cb:n_cb + GROUP_CH], big[:, n_cb + GROUP_CH:]
            u8 = u_scr[ci]
            dt8 = dt_scr[ci]
            cm = cc[pl.ds(r0, CHUNK), :]
            g = jnp.dot(cm, btc, preferred_element_type=F32)
            xs_b = xs.astype(BF16)
            w_parts, x_parts = [], []
            for r in range(nh):
                ef = cb[:, r * CHUNK:(r + 1) * CHUNK] - u8[r:r + 1, :]
                eb = u8[nh + r:nh + r + 1, :] - cb[:, (nh + r) * CHUNK:(nh + r + 1) * CHUNK]
                df = jnp.exp(jnp.where(lower, ef, NEG_BIG)) * dt8[r:r + 1, :]
                db = jnp.exp(jnp.where(upper, eb, NEG_BIG)) * dt8[nh + r:nh + r + 1, :]
                w_parts.append((g * (df + db)).astype(BF16))
                x_parts.append(jnp.where(lane_head == r, xs_b, jnp.zeros_like(xs_b)))
            y = jnp.dot(jnp.concatenate(w_parts, axis=1), jnp.concatenate(x_parts, axis=0),
                        preferred_element_type=F32)
            y = y + e_col * jnp.dot(cm, st_prev.astype(BF16), preferred_element_type=F32)
            y_scr[pl.ds(y_off_row, CHUNK), :] = y + dskip * xs
        xw = (xs * w_col).astype(BF16)
        return st_prev * e_col[CHUNK - 1:CHUNK, :] + jnp.dot(btc, xw, preferred_element_type=F32)

    def bwd_chunk(ci, y_off_row, st_prev):
        r0 = pl.multiple_of(ci * CHUNK, CHUNK)
        xt = xt_scr[:, pl.ds(r0, CHUNK)]
        xs = xc[pl.ds(r0, CHUNK), :]
        btc = bt[:, pl.ds(r0, CHUNK)]
        big = jnp.dot(xt, rmat_b, preferred_element_type=F32)
        w_col, e_col = big[:, :GROUP_CH], big[:, GROUP_CH:]
        if y_off_row is not None:
            cm = cc[pl.ds(r0, CHUNK), :]
            y = y_scr[pl.ds(y_off_row, CHUNK), :]
            y = y + e_col * jnp.dot(cm, st_prev.astype(BF16), preferred_element_type=F32)
            gated = y * _silu(z_ref[pl.ds(y_off_row, CHUNK), :].astype(F32))
            ms = jnp.mean(gated * gated, axis=-1, keepdims=True)
            o_ref[pl.ds(y_off_row, CHUNK), :] = ((gated * lax.rsqrt(ms + EPS)) * ng_ref[...]).astype(o_ref.dtype)
        xw = (xs * w_col).astype(BF16)
        return st_prev * e_col[0:1, :] + jnp.dot(btc, xw, preferred_element_type=F32)

    unroll = math.gcd(SSD_UNROLL, n_lat_chunks)

    st = jnp.zeros((D_STATE, GROUP_CH), F32)
    for c in range(n_ctx_chunks):
        st = fwd_chunk(c, None, st)
    st_f[...] = st

    def fwd_body(i, carry):
        s = st_f[...]
        for u in range(unroll):
            c = i * unroll + u
            s = fwd_chunk(n_ctx_chunks + c, pl.multiple_of(c * CHUNK, CHUNK), s)
        st_f[...] = s
        return carry

    lax.fori_loop(0, n_lat_chunks // unroll, fwd_body, 0)

    st = jnp.zeros((D_STATE, GROUP_CH), F32)
    for c in reversed(range(n_ctx_chunks)):
        st = bwd_chunk(c, None, st)
    st_b[...] = st

    def bwd_body(i, carry):
        s = st_b[...]
        for u in range(unroll):
            c = n_lat_chunks - 1 - (i * unroll + u)
            s = bwd_chunk(n_ctx_chunks + c, pl.multiple_of(c * CHUNK, CHUNK), s)
        st_b[...] = s
        return carry

    lax.fori_loop(0, n_lat_chunks // unroll, bwd_body, 0)


def _ssd(xbc_all, conv_w, conv_b, dt_c, aneg_col, dskip_row, z, norm_g, batch, seq, n_ctx):
    d_ssm = SSM_HEADS * SSM_HEAD_DIM
    b0 = d_ssm // D_STATE
    c0 = b0 + SSM_GROUPS
    ctx_blk0 = batch * seq // n_ctx
    n_ctx_chunks, n_lat_chunks = n_ctx // CHUNK, seq // CHUNK
    n_tok = n_ctx + seq
    in_specs = [
        pl.BlockSpec((seq, GROUP_CH), lambda b, g: (b, g)),
        pl.BlockSpec((seq, D_STATE), lambda b, g: (b, b0 + g)),
        pl.BlockSpec((seq, D_STATE), lambda b, g: (b, c0 + g)),
        pl.BlockSpec((n_ctx, GROUP_CH), lambda b, g: (ctx_blk0 + b, g)),
        pl.BlockSpec((n_ctx, D_STATE), lambda b, g: (ctx_blk0 + b, b0 + g)),
        pl.BlockSpec((n_ctx, D_STATE), lambda b, g: (ctx_blk0 + b, c0 + g)),
        pl.BlockSpec((CONV_W, GROUP_CH), lambda b, g: (0, g)),
        pl.BlockSpec((CONV_W, D_STATE), lambda b, g: (0, b0 + g)),
        pl.BlockSpec((CONV_W, D_STATE), lambda b, g: (0, c0 + g)),
        pl.BlockSpec((1, GROUP_CH), lambda b, g: (0, g)),
        pl.BlockSpec((1, D_STATE), lambda b, g: (0, b0 + g)),
        pl.BlockSpec((1, D_STATE), lambda b, g: (0, c0 + g)),
        pl.BlockSpec((n_lat_chunks, 8, CHUNK), lambda b, g: (b, g, 0)),
        pl.BlockSpec((n_ctx_chunks, 8, CHUNK), lambda b, g: (ctx_blk0 + b, g, 0)),
        pl.BlockSpec((8, 1), lambda b, g: (g, 0)),
        pl.BlockSpec((1, GROUP_CH), lambda b, g: (0, g)),
        pl.BlockSpec((seq, GROUP_CH), lambda b, g: (b, g)),
        pl.BlockSpec((1, GROUP_CH), lambda b, g: (0, g)),
    ]
    return pl.pallas_call(
        functools.partial(_ssd_kernel, seq=seq, n_ctx=n_ctx),
        grid=(batch, SSM_GROUPS),
        in_specs=in_specs,
        out_specs=pl.BlockSpec((seq, GROUP_CH), lambda b, g: (b, g)),
        out_shape=jax.ShapeDtypeStruct((batch * seq, d_ssm), BF16),
        scratch_shapes=[pltpu.VMEM((n_tok, GROUP_CH), F32),
                        pltpu.VMEM((D_STATE, n_tok), BF16),
                        pltpu.VMEM((n_tok, D_STATE), BF16),
                        pltpu.VMEM((seq, GROUP_CH), F32),
                        pltpu.VMEM((n_tok // CHUNK, 8, CHUNK), F32),
                        pltpu.VMEM((n_tok // CHUNK, 8, CHUNK), F32),
                        pltpu.VMEM((CHUNK, n_tok), BF16),
                        pltpu.VMEM((D_STATE, GROUP_CH), F32),
                        pltpu.VMEM((D_STATE, GROUP_CH), F32)],
        compiler_params=_cparams(2),
        name="ssd",
    )(xbc_all, xbc_all, xbc_all, xbc_all, xbc_all, xbc_all,
      conv_w, conv_w, conv_w, conv_b, conv_b, conv_b,
      dt_c, dt_c, aneg_col, dskip_row, z, norm_g)


def _rope_tables(seq):
    n_freq = HEAD_DIM // 4
    pos = np.arange(seq)
    inv = np.power(ROPE_THETA, -np.arange(n_freq, dtype=np.float64) / n_freq)
    ang_r = (pos // GRID_W)[:, None] * inv
    ang_c = (pos % GRID_W)[:, None] * inv
    cos_t = np.concatenate([np.cos(ang_r), np.cos(ang_r), np.cos(ang_c), np.cos(ang_c)], axis=1)
    sin_t = np.concatenate([-np.sin(ang_r), np.sin(ang_r), -np.sin(ang_c), np.sin(ang_c)], axis=1)
    return jnp.asarray(cos_t, F32), jnp.asarray(sin_t, F32)


def kernel(x, c, ctx, c_ctx, w_mod, b_mod, norm1, w_in, conv_w, conv_b, dt_bias, a_log, d_skip,
           ssm_norm, q_norm, k_norm, w_ssm_br, w_attn_br, w_o, norm2, w_ffn_in, w_ffn_out):
    batch, seq, d = x.shape
    n_ctx = ctx.shape[1]
    n_lat = batch * seq
    n_ctx_rows = batch * n_ctx
    n_all = n_lat + n_ctx_rows
    d_ssm = SSM_HEADS * SSM_HEAD_DIM
    d_conv = d_ssm + 2 * SSM_GROUPS * D_STATE
    d_attn = ATTN_HEADS * HEAD_DIM
    d_kv = KV_HEADS * HEAD_DIM
    d_ff = w_ffn_out.shape[1]
    n_dt = 2 * SSM_HEADS
    o_xbc = d_ssm
    o_dt = o_xbc + d_conv
    o_q = o_dt + n_dt
    o_k = o_q + d_attn
    o_g = o_k + 2 * d_kv
    assert w_mod.shape[0] == 1, "single-layer block"

    cvec = jnp.concatenate([c, c_ctx[None, :], jnp.zeros((8 - batch - 1, d), F32)], axis=0)
    mod = _modulation(cvec, w_mod[0], b_mod[0][None, :])
    mod3 = mod[:batch + 1].reshape(batch + 1, 1, N_MOD * d)

    w_in_t = jnp.transpose(w_in[0])
    perm = np.array([dr * SSM_HEADS + g * HEADS_PER_GROUP + r
                     for g in range(SSM_GROUPS) for dr in range(2) for r in range(HEADS_PER_GROUP)])
    w_dt_t = w_in_t[o_dt:o_dt + n_dt][perm].astype(BF16)
    dt_bias_col = dt_bias[0].reshape(n_dt)[perm][:, None]
    aneg_col = (-jnp.exp(a_log[0].astype(F32))).reshape(n_dt)[perm][:, None]

    x2d = x.reshape(n_lat, d)
    h_all, dt_c = _prenorm1(x2d, ctx.reshape(n_ctx_rows, d), norm1, mod3, batch, 0, 1, w_dt_t, dt_bias_col)

    tn = 1024
    tm_lat = 1024 if seq % 1024 == 0 else 512
    tm_all = n_all // 8 if (n_all // 8) % 16 == 0 and n_all % 8 == 0 else 512

    def in_proj(sections, rows, tm_rows, name):
        starts = np.cumsum([0] + [w // tn for _, w in sections])

        def row_off(j):
            off = sections[0][0] + j * tn
            for (first, _), s in zip(sections[1:], starts[1:]):
                off = jnp.where(j >= s, first + (j - s) * tn, off)
            return off

        return _matmul([(h_all, 0)], [(w_in_t, row_off, 0)], [], _ep_plain, int(starts[-1]) * tn, BF16,
                       m_rows=rows, tm=tm_rows, tn=tn, name=name, w_t=True)

    p_lat = in_proj([(0, d_ssm), (o_q, d_attn), (o_g, 2 * d)], n_lat, tm_lat, "proj_latent")
    p_all = in_proj([(o_xbc, d_conv), (o_k, 2 * d_kv)], n_all, tm_all, "proj_all")
    c_q = d_ssm
    c_g = d_ssm + d_attn
    c_k = d_conv
    c_v = d_conv + d_kv

    dskip_row = jnp.repeat(d_skip[0].astype(F32), SSM_HEAD_DIM)[None, :]
    y_norm = _ssd(p_all, conv_w[0], conv_b[0][None, :], dt_c, aneg_col, dskip_row, p_lat,
                  ssm_norm, batch, seq, n_ctx)

    cos_t, sin_t = _rope_tables(seq)
    attn = _attention(p_lat, c_q, p_all, c_k, c_v, q_norm, k_norm, cos_t, sin_t, batch, seq, n_ctx)

    tn_m = 512
    merged = _matmul([(y_norm, 0), (attn, 0)], [(w_ssm_br[0], 0, 0), (w_attn_br[0], 0, 1)],
                     [(p_lat, (tm_lat, tn_m), lambda j, m: (m, c_g // tn_m + j)),
                      (p_lat, (tm_lat, tn_m), lambda j, m: (m, (c_g + d) // tn_m + j))],
                     _ep_merge, d, BF16, m_rows=n_lat, tm=tm_lat, tn=tn_m, name="branch_merge")
    x_mid, h2 = _out_proj(merged, w_o[0], x2d, norm2, mod3, batch, 2, 3, 4)

    tn_f = 512
    act = _matmul([(h2, 0)], [(w_ffn_in[0], 0, 0), (w_ffn_in[0], d_ff // tn_f, 0)], [], _ep_swiglu,
                  d_ff, BF16, m_rows=n_lat, tm=tm_lat, tn=tn_f, name="ffn_in")
    tm_o = 512
    tiles_per_batch_o = seq // tm_o
    out = _matmul([(act, 0)], [(w_ffn_out[0], 0, 0)],
                  [(x_mid, (tm_o, tn_f), lambda j, m: (m, j)),
                   (mod3, (1, 1, tn_f), lambda j, m: (m // tiles_per_batch_o, 0, 5 * (d // tn_f) + j))],
                  _ep_gated_residual, d, F32, m_rows=n_lat, tm=tm_o, tn=tn_f, name="ffn_out")
    return out.reshape(batch, seq, d)
```

```python
import functools
import math

import jax
import jax.numpy as jnp
import numpy as np
from jax import lax
from jax.experimental import pallas as pl
from jax.experimental.pallas import tpu as pltpu

F32 = jnp.float32
BF16 = jnp.bfloat16

GRID_W = 64
SSM_HEADS = 32
SSM_HEAD_DIM = 64
SSM_GROUPS = 8
D_STATE = 128
CONV_W = 5
CHUNK = 128
ATTN_HEADS = 16
KV_HEADS = 4
HEAD_DIM = 128
ROPE_THETA = 10000.0
N_MOD = 6
EPS = 1e-6

HEADS_PER_GROUP = SSM_HEADS // SSM_GROUPS
GROUP_CH = HEADS_PER_GROUP * SSM_HEAD_DIM
Q_REP = ATTN_HEADS // KV_HEADS

VMEM_LIMIT_BYTES = 56 * 1024 * 1024
LOG2_E = 1.4426950408889634
SSD_UNROLL = 16


def _cparams(n_grid):
    return pltpu.CompilerParams(dimension_semantics=("arbitrary",) * n_grid,
                                vmem_limit_bytes=VMEM_LIMIT_BYTES)


def _silu(v):
    return v * jax.nn.sigmoid(v)


def _mod_kernel(c_ref, w_ref, b_ref, o_ref):
    s = _silu(c_ref[...]).astype(BF16)
    o_ref[...] = jnp.dot(s, w_ref[...].astype(BF16), preferred_element_type=F32) + b_ref[...]


def _modulation(cvec, w_mod, b_mod):
    rows, d = cvec.shape
    n = w_mod.shape[1]
    tn = 1024
    return pl.pallas_call(
        _mod_kernel,
        grid=(n // tn,),
        in_specs=[pl.BlockSpec((rows, d), lambda j: (0, 0)),
                  pl.BlockSpec((d, tn), lambda j: (0, j)),
                  pl.BlockSpec((1, tn), lambda j: (0, j))],
        out_specs=pl.BlockSpec((rows, tn), lambda j: (0, j)),
        out_shape=jax.ShapeDtypeStruct((rows, n), F32),
        compiler_params=_cparams(1),
        name="modulation",
    )(cvec, w_mod, b_mod)


PRENORM_ROWS = 512


NORM_ROWS = 16


def _norm_mod(t, g, sh, sc):
    ms = jnp.mean(t * t, axis=-1, keepdims=True)
    return ((t * lax.rsqrt(ms + EPS)) * g) * (1.0 + sc) + sh


def _norm_mod_rows(src_ref, dst_ref, g_ref, sh_ref, sc_ref):
    g = g_ref[...]
    sh = sh_ref[0]
    sc = sc_ref[0]

    def body(k, carry):
        rows = pl.ds(pl.multiple_of(k * NORM_ROWS, NORM_ROWS), NORM_ROWS)
        dst_ref[rows, :] = _norm_mod(src_ref[rows, :], g, sh, sc).astype(dst_ref.dtype)
        return carry

    lax.fori_loop(0, src_ref.shape[0] // NORM_ROWS, body, 0, unroll=8)


def _prenorm1_kernel(x_ref, ctx_ref, g_ref, sh_ref, sc_ref, wdt_ref, bdt_ref, o_ref, dt_ref, *, n_lat_tiles):
    i = pl.program_id(0)

    @pl.when(i < n_lat_tiles)
    def _():
        _norm_mod_rows(x_ref, o_ref, g_ref, sh_ref, sc_ref)

    @pl.when(i >= n_lat_tiles)
    def _():
        _norm_mod_rows(ctx_ref, o_ref, g_ref, sh_ref, sc_ref)

    raw = lax.dot_general(wdt_ref[...], o_ref[...], (((1,), (1,)), ((), ())),
                          preferred_element_type=F32) + bdt_ref[...]
    dt = jnp.maximum(raw, 0.0) + jnp.log1p(jnp.exp(-jnp.abs(raw)))
    for c in range(dt_ref.shape[0]):
        dt_ref[c] = dt[:, c * CHUNK:(c + 1) * CHUNK]


def _prenorm1(x2d, ctx2d, norm_g, mod3, batch, shift_blk, scale_blk, w_dt_t, dt_bias_col):
    n_lat, d = x2d.shape
    n_ctx = ctx2d.shape[0]
    n_dt = w_dt_t.shape[0]
    tr = math.gcd(PRENORM_ROWS, math.gcd(n_lat // batch, n_ctx))
    lat_tiles, ctx_tiles = n_lat // tr, n_ctx // tr
    tiles_per_batch = lat_tiles // batch

    def mod_row(i):
        return jnp.where(i < lat_tiles, i // tiles_per_batch, batch)

    return pl.pallas_call(
        functools.partial(_prenorm1_kernel, n_lat_tiles=lat_tiles),
        grid=(lat_tiles + ctx_tiles,),
        in_specs=[pl.BlockSpec((tr, d), lambda i: (jnp.minimum(i, lat_tiles - 1), 0)),
                  pl.BlockSpec((tr, d), lambda i: (jnp.maximum(i - lat_tiles, 0), 0)),
                  pl.BlockSpec((1, d), lambda i: (0, 0)),
                  pl.BlockSpec((1, 1, d), lambda i: (mod_row(i), 0, shift_blk)),
                  pl.BlockSpec((1, 1, d), lambda i: (mod_row(i), 0, scale_blk)),
                  pl.BlockSpec((n_dt, d), lambda i: (0, 0)),
                  pl.BlockSpec((n_dt, 1), lambda i: (0, 0))],
        out_specs=[pl.BlockSpec((tr, d), lambda i: (i, 0)),
                   pl.BlockSpec((tr // CHUNK, n_dt, CHUNK), lambda i: (i, 0, 0))],
        out_shape=[jax.ShapeDtypeStruct((n_lat + n_ctx, d), BF16),
                   jax.ShapeDtypeStruct(((n_lat + n_ctx) // CHUNK, n_dt, CHUNK), F32)],
        compiler_params=_cparams(1),
        name="prenorm1",
    )(x2d, ctx2d, norm_g, mod3, mod3, w_dt_t, dt_bias_col)


OUT_PROJ_ROWS = 512
OUT_PROJ_SECTIONS = 2


def _out_proj_kernel(a_ref, w_ref, x_ref, gt_ref, g_ref, sh_ref, sc_ref, xmid_ref, h_ref, w_scr):
    @pl.when(pl.program_id(0) == 0)
    def _():
        w_scr[...] = w_ref[...].astype(BF16)

    half = a_ref.shape[0] // OUT_PROJ_SECTIONS
    g, sh, sc, gt = g_ref[...], sh_ref[0], sc_ref[0], gt_ref[0]
    for h0 in range(0, a_ref.shape[0], half):
        xmid_ref[h0:h0 + half, :] = x_ref[h0:h0 + half, :] + gt * jnp.dot(
            a_ref[h0:h0 + half, :], w_scr[...], preferred_element_type=F32)
        for r in range(h0, h0 + half, NORM_ROWS):
            h_ref[r:r + NORM_ROWS, :] = _norm_mod(xmid_ref[r:r + NORM_ROWS, :], g, sh, sc).astype(h_ref.dtype)


def _out_proj(a, w, x2d, norm_g, mod3, batch, gate_blk, shift_blk, scale_blk):
    n_lat, d = x2d.shape
    tr = math.gcd(OUT_PROJ_ROWS, n_lat // batch)
    tiles_per_batch = n_lat // tr // batch

    def mod_spec(blk):
        return pl.BlockSpec((1, 1, d), lambda i: (i // tiles_per_batch, 0, blk))

    return pl.pallas_call(
        _out_proj_kernel,
        grid=(n_lat // tr,),
        in_specs=[pl.BlockSpec((tr, a.shape[1]), lambda i: (i, 0)),
                  pl.BlockSpec(w.shape, lambda i: (0, 0), pipeline_mode=pl.Buffered(1)),
                  pl.BlockSpec((tr, d), lambda i: (i, 0)),
                  mod_spec(gate_blk),
                  pl.BlockSpec((1, d), lambda i: (0, 0)),
                  mod_spec(shift_blk),
                  mod_spec(scale_blk)],
        out_specs=[pl.BlockSpec((tr, d), lambda i: (i, 0)),
                   pl.BlockSpec((tr, d), lambda i: (i, 0))],
        out_shape=[jax.ShapeDtypeStruct((n_lat, d), F32),
                   jax.ShapeDtypeStruct((n_lat, d), BF16)],
        scratch_shapes=[pltpu.VMEM(w.shape, BF16)],
        compiler_params=_cparams(1),
        name="out_proj",
    )(a, w, x2d, mod3, norm_g, mod3, mod3)


def _mm_kernel(*refs, n_a, which_a, n_extra, epilogue, w_t):
    n_w = len(which_a)
    a_refs = refs[:n_a]
    w_refs = refs[n_a:n_a + n_w]
    e_refs = refs[n_a + n_w:n_a + n_w + n_extra]
    o_ref = refs[n_a + n_w + n_extra]
    w_scr = refs[n_a + n_w + n_extra + 1:]

    @pl.when(pl.program_id(1) == 0)
    def _():
        for k in range(n_w):
            w_scr[k][...] = w_refs[k][...].astype(BF16)

    dims = (((1,), (1,)), ((), ())) if w_t else (((1,), (0,)), ((), ()))
    accs = [lax.dot_general(a_refs[which_a[k]][...], w_scr[k][...], dims, preferred_element_type=F32)
            for k in range(n_w)]
    o_ref[...] = epilogue(accs, e_refs).astype(o_ref.dtype)


def _matmul(a_list, w_list, extras, epilogue, n_out, out_dtype, *, m_rows, tm, tn, name, w_t=False):
    grid = (n_out // tn, m_rows // tm)
    in_specs, args, scratch = [], [], []
    for a, off in a_list:
        in_specs.append(pl.BlockSpec((tm, a.shape[1]), lambda j, m, off=off: (m + off, 0)))
        args.append(a)
    for w, off, _ in w_list:
        if w_t:
            in_specs.append(pl.BlockSpec((pl.Element(tn), pl.Element(w.shape[1])),
                                         lambda j, m, off=off: (pl.multiple_of(off(j), 8), 0)))
            scratch.append(pltpu.VMEM((tn, w.shape[1]), BF16))
        else:
            in_specs.append(pl.BlockSpec((w.shape[0], tn), lambda j, m, off=off: (0, j + off)))
            scratch.append(pltpu.VMEM((w.shape[0], tn), BF16))
        args.append(w)
    for e, blk, imap in extras:
        in_specs.append(pl.BlockSpec(blk, imap))
        args.append(e)
    kern = functools.partial(_mm_kernel, n_a=len(a_list), which_a=tuple(w[2] for w in w_list),
                             n_extra=len(extras), epilogue=epilogue, w_t=w_t)
    return pl.pallas_call(
        kern,
        grid=grid,
        in_specs=in_specs,
        out_specs=pl.BlockSpec((tm, tn), lambda j, m: (m, j)),
        out_shape=jax.ShapeDtypeStruct((m_rows, n_out), out_dtype),
        scratch_shapes=scratch,
        compiler_params=_cparams(2),
        name=name,
    )(*args)


def _ep_plain(accs, e_refs):
    return accs[0]


def _ep_swiglu(accs, e_refs):
    return _silu(accs[0]) * accs[1]


def _ep_merge(accs, e_refs):
    g_ssm = e_refs[0][...].astype(F32)
    g_attn = e_refs[1][...].astype(F32)
    return jax.nn.sigmoid(g_ssm) * accs[0] + jax.nn.sigmoid(g_attn) * accs[1]


def _ep_gated_residual(accs, e_refs):
    return e_refs[0][...] + e_refs[1][0] * accs[0]


def _norm_rope(t, g, cs, sn, scale):
    tn = (t * lax.rsqrt(jnp.mean(t * t, axis=-1, keepdims=True) + EPS)) * g
    lane = lax.broadcasted_iota(jnp.int32, tn.shape, 1)
    first_half = (lane % (HEAD_DIM // 2)) < (HEAD_DIM // 4)
    rot = jnp.where(first_half,
                    pltpu.roll(tn, HEAD_DIM - HEAD_DIM // 4, 1),
                    pltpu.roll(tn, HEAD_DIM // 4, 1))
    return (tn * cs + rot * sn) * scale


def _attn_kernel(q_ref, kl_ref, kc_ref, vl_ref, vc_ref, qg_ref, kg_ref, cos_ref, sin_ref, o_ref,
                 k_scr, v_scr, s0_scr, s1_scr, p0_scr, p1_scr, *, n_ctx, tq, q_scale):
    seq = q_ref.shape[0]
    n_tiles = seq // tq
    kg = kg_ref[...]
    qg = qg_ref[...]

    kc = kc_ref[...].astype(F32)
    k_scr[0:n_ctx, :] = ((kc * lax.rsqrt(jnp.mean(kc * kc, axis=-1, keepdims=True) + EPS)) * kg).astype(BF16)

    def k_body(i, carry):
        r0 = pl.multiple_of(i * tq, tq)
        kt = _norm_rope(kl_ref[pl.ds(r0, tq), :].astype(F32), kg,
                        cos_ref[pl.ds(r0, tq), :], sin_ref[pl.ds(r0, tq), :], 1.0)
        k_scr[pl.ds(pl.multiple_of(n_ctx + r0, tq), tq), :] = kt.astype(BF16)
        return carry

    lax.fori_loop(0, n_tiles, k_body, 0, unroll=2)
    v_scr[0:n_ctx, 0:HEAD_DIM] = vc_ref[...]
    v_scr[n_ctx:, 0:HEAD_DIM] = vl_ref[...]
    v_scr[:, HEAD_DIM:] = jnp.ones((v_scr.shape[0], HEAD_DIM), BF16)

    s_bufs = (s0_scr, s1_scr)
    p_bufs = (p0_scr, p1_scr)

    def scores(t, r):
        r0 = pl.multiple_of(t * tq, tq)
        q = _norm_rope(q_ref[pl.ds(r0, tq), r * HEAD_DIM:(r + 1) * HEAD_DIM].astype(F32), qg,
                       cos_ref[pl.ds(r0, tq), :], sin_ref[pl.ds(r0, tq), :], q_scale).astype(BF16)
        s_bufs[r % 2][...] = lax.dot_general(q, k_scr[...], (((1,), (1,)), ((), ())),
                                             preferred_element_type=F32)

    def softmax(r):
        sc = s_bufs[r % 2][...]
        mx = jnp.max(sc, axis=-1, keepdims=True)
        p_bufs[r % 2][...] = jnp.exp2((sc - mx).astype(BF16))

    def values(t, r):
        r0 = pl.multiple_of(t * tq, tq)
        o = jnp.dot(p_bufs[r % 2][...], v_scr[...], preferred_element_type=F32)
        den = o[:, HEAD_DIM:HEAD_DIM + 1]
        o_ref[pl.ds(r0, tq), r * HEAD_DIM:(r + 1) * HEAD_DIM] = (o[:, :HEAD_DIM] / den).astype(o_ref.dtype)

    def tile(t, first):
        for r in range(Q_REP):
            if r >= 1:
                softmax(r - 1)
            elif not first:
                softmax(Q_REP - 1)
            if r >= 2:
                values(t, r - 2)
            elif not first:
                values(t - 1, Q_REP - 2 + r)
            scores(t, r)

    tile(0, True)

    def body(t, carry):
        tile(t, False)
        return carry

    lax.fori_loop(1, n_tiles, body, 0)
    softmax(Q_REP - 1)
    values(n_tiles - 1, Q_REP - 2)
    values(n_tiles - 1, Q_REP - 1)


def _attention(p_lat, c_q, p_all, c_k, c_v, q_norm, k_norm, cos_t, sin_t, batch, seq, n_ctx):
    tq = 256
    assert Q_REP % 2 == 0 and Q_REP >= 2
    ctx_blk0 = batch * seq // n_ctx
    gw = Q_REP * HEAD_DIM
    n_keys = n_ctx + seq
    qb, kb, vb = c_q // gw, c_k // HEAD_DIM, c_v // HEAD_DIM
    const = dict(pipeline_mode=pl.Buffered(1))
    return pl.pallas_call(
        functools.partial(_attn_kernel, n_ctx=n_ctx, tq=tq, q_scale=HEAD_DIM ** -0.5 * LOG2_E),
        grid=(batch, KV_HEADS),
        in_specs=[pl.BlockSpec((seq, gw), lambda b, g: (b, qb + g)),
                  pl.BlockSpec((seq, HEAD_DIM), lambda b, g: (b, kb + g)),
                  pl.BlockSpec((n_ctx, HEAD_DIM), lambda b, g: (ctx_blk0 + b, kb + g)),
                  pl.BlockSpec((seq, HEAD_DIM), lambda b, g: (b, vb + g)),
                  pl.BlockSpec((n_ctx, HEAD_DIM), lambda b, g: (ctx_blk0 + b, vb + g)),
                  pl.BlockSpec((1, HEAD_DIM), lambda b, g: (0, 0)),
                  pl.BlockSpec((1, HEAD_DIM), lambda b, g: (0, 0)),
                  pl.BlockSpec((seq, HEAD_DIM), lambda b, g: (0, 0), **const),
                  pl.BlockSpec((seq, HEAD_DIM), lambda b, g: (0, 0), **const)],
        out_specs=pl.BlockSpec((seq, gw), lambda b, g: (b, g)),
        out_shape=jax.ShapeDtypeStruct((batch * seq, ATTN_HEADS * HEAD_DIM), BF16),
        scratch_shapes=[pltpu.VMEM((n_keys, HEAD_DIM), BF16),
                        pltpu.VMEM((n_keys, 2 * HEAD_DIM), BF16),
                        pltpu.VMEM((tq, n_keys), F32),
                        pltpu.VMEM((tq, n_keys), F32),
                        pltpu.VMEM((tq, n_keys), BF16),
                        pltpu.VMEM((tq, n_keys), BF16)],
        compiler_params=_cparams(2),
        name="attention",
    )(p_lat, p_all, p_all, p_all, p_all, q_norm, k_norm, cos_t, sin_t)


def _split3(v):
    hi = v.astype(BF16).astype(F32)
    r1 = v - hi
    mid = r1.astype(BF16).astype(F32)
    lo = (r1 - mid).astype(BF16).astype(F32)
    return hi, mid, lo


CONV_HALO = 16


def _conv_shift_matrix():
    pad = (CONV_W - 1) // 2
    taps = [j for j in range(CONV_W) if j != pad]
    rows = lax.broadcasted_iota(jnp.int32, (len(taps) * CHUNK, CHUNK + 2 * CONV_HALO), 0)
    cols = lax.broadcasted_iota(jnp.int32, (len(taps) * CHUNK, CHUNK + 2 * CONV_HALO), 1)
    tap = jnp.where(rows // CHUNK < pad, rows // CHUNK, rows // CHUNK + 1)
    return (cols == CONV_HALO - pad + tap + rows % CHUNK).astype(BF16), taps


def _conv_block_fn(streams, n_rows, shift, taps):
    n_blk = n_rows // CHUNK
    pad = (CONV_W - 1) // 2

    def lanes(parts):
        return parts[0] if len(parts) == 1 else jnp.concatenate(parts, axis=1)

    def block(i):
        r0 = pl.multiple_of(i * CHUNK, CHUNK)
        lo = pl.multiple_of(jnp.maximum(r0 - CONV_HALO, 0), CONV_HALO)
        hi = pl.multiple_of(jnp.minimum(r0 + CHUNK, n_rows - CONV_HALO), CONV_HALO)
        out = []
        for src_refs, w_refs, b_refs, store in streams:
            cur = lanes([s[pl.ds(r0, CHUNK), :] for s in src_refs])
            prev = lanes([s[pl.ds(lo, CONV_HALO), :] for s in src_refs])
            nxt = lanes([s[pl.ds(hi, CONV_HALO), :] for s in src_refs])
            prev = jnp.where(i > 0, prev, jnp.zeros_like(prev))
            nxt = jnp.where(i < n_blk - 1, nxt, jnp.zeros_like(nxt))
            window = jnp.concatenate([prev, cur, nxt], axis=0)
            shifted = jnp.dot(shift, window, preferred_element_type=F32)
            w = lanes([r[...] for r in w_refs])
            acc = lanes([r[...] for r in b_refs]) + w[pad:pad + 1, :] * cur.astype(F32)
            for k, j in enumerate(taps):
                acc = acc + w[j:j + 1, :] * shifted[k * CHUNK:(k + 1) * CHUNK, :]
            out.append(store(r0, _silu(acc)))
        return out

    return block


def _ssd_kernel(x_lat, b_lat, c_lat, x_ctx, b_ctx, c_ctx,
                cwx, cwb, cwc, cbx, cbb, cbc,
                dt_lat, dt_ctx, aneg_ref, dskip_ref, z_ref, ng_ref,
                o_ref,
                xc, bt, cc, y_scr, dt_scr, u_scr, xt_scr, st_f, st_b, *, seq, n_ctx):
    nh = HEADS_PER_GROUP
    n_ctx_chunks = n_ctx // CHUNK
    n_lat_chunks = seq // CHUNK
    n_chunks = n_ctx_chunks + n_lat_chunks

    def store_x(off):
        def f(r0, v):
            xc[pl.ds(pl.multiple_of(off + r0, CHUNK), CHUNK), :] = v
            return v
        return f

    def store_bc(off):
        def f(r0, v):
            r = pl.multiple_of(off + r0, CHUNK)
            b_t = v[:, :D_STATE].T.astype(BF16)
            c_m = v[:, D_STATE:].astype(BF16)
            bt[:, pl.ds(r, CHUNK)] = b_t
            cc[pl.ds(r, CHUNK), :] = c_m
            return b_t, c_m
        return f

    shift, taps = _conv_shift_matrix()
    conv_ctx = _conv_block_fn([([x_ctx], [cwx], [cbx], store_x(0)),
                               ([b_ctx, c_ctx], [cwb, cwc], [cbb, cbc], store_bc(0))], n_ctx, shift, taps)
    conv_lat = _conv_block_fn([([x_lat], [cwx], [cbx], store_x(n_ctx)),
                               ([b_lat, c_lat], [cwb, cwc], [cbb, cbc], store_bc(n_ctx))], seq, shift, taps)
    for c in range(n_ctx_chunks):
        conv_ctx(c)

    row = lax.broadcasted_iota(jnp.int32, (CHUNK, CHUNK), 0)
    col = lax.broadcasted_iota(jnp.int32, (CHUNK, CHUNK), 1)
    triu = (row <= col).astype(BF16)
    eye = (row == col).astype(BF16)
    lower = row >= col
    below = row > col
    above = row < col
    dskip = dskip_ref[...]
    lane_head = lax.broadcasted_iota(jnp.int32, (CHUNK, GROUP_CH), 1) // SSM_HEAD_DIM

    def spread(n_cols, q_of_col, hd_of_col):
        k = lax.broadcasted_iota(jnp.int32, (CHUNK, n_cols), 0)
        c = lax.broadcasted_iota(jnp.int32, (CHUNK, n_cols), 1)
        return ((k < 72) & (k // 24 == q_of_col(c)) & (k % 8 == hd_of_col(c))).astype(BF16)

    n_cb = 2 * nh * CHUNK
    rmat_f = spread(n_cb + 2 * GROUP_CH,
                    lambda c: jnp.where(c < n_cb, 0, jnp.where(c < n_cb + GROUP_CH, 1, 2)),
                    lambda c: jnp.where(c < n_cb, c // CHUNK, ((c - n_cb) % GROUP_CH) // SSM_HEAD_DIM))
    rmat_b = spread(2 * GROUP_CH,
                    lambda c: jnp.where(c < GROUP_CH, 1, 2),
                    lambda c: nh + (c % GROUP_CH) // SSM_HEAD_DIM)

    dt_scr[0:n_ctx_chunks] = dt_ctx[...]
    dt_scr[n_ctx_chunks:n_chunks] = dt_lat[...]
    dt3 = dt_scr[...]
    n_rows = n_chunks * 8
    dt2 = dt3.reshape(n_rows, CHUNK)
    a2 = (dt3 * aneg_ref[...][None]).reshape(n_rows, CHUNK)
    cum = sum(jnp.dot(part.astype(BF16), triu, preferred_element_type=F32) for part in _split3(a2))
    total = jnp.broadcast_to(cum[:, CHUNK - 1:CHUNK], cum.shape)
    cumx = cum - a2
    is_fwd = (lax.broadcasted_iota(jnp.int32, (n_rows, CHUNK), 0) % 8) < nh
    u2 = jnp.where(is_fwd, cum, cumx) * LOG2_E
    w2 = dt2 * jnp.exp(jnp.where(is_fwd, total - cum, cumx))
    e2 = jnp.exp(jnp.where(is_fwd, cum, total - cumx))
    u_scr[...] = u2.reshape(n_chunks, 8, CHUNK)
    pieces = [p.reshape(n_chunks, 8, CHUNK) for v in (u2, w2, e2) for p in _split3(v)]
    pieces.append(jnp.zeros((n_chunks, CHUNK - 8 * len(pieces), CHUNK), F32))
    packed = jnp.concatenate(pieces, axis=1).reshape(n_chunks * CHUNK, CHUNK).astype(BF16)
    xt_scr[...] = lax.dot_general(eye, packed, (((1,), (1,)), ((), ())),
                                  preferred_element_type=F32).astype(BF16)

    def fwd_chunk(ci, y_off_row, st_prev, conv_vals=None):
        r0 = pl.multiple_of(ci * CHUNK, CHUNK)
        xt = xt_scr[:, pl.ds(r0, CHUNK)]
        if conv_vals is None:
            xs = xc[pl.ds(r0, CHUNK), :]
            btc = bt[:, pl.ds(r0, CHUNK)]
        else:
            xs, (btc, cm) = conv_vals
        if y_off_row is None:
            big = jnp.dot(xt, rmat_f[:, n_cb:], preferred_element_type=F32)
            w_col, e_col = big[:, :GROUP_CH], big[:, GROUP_CH:]
        else:
            big = jnp.dot(xt, rmat_f, preferred_element_type=F32)
            cb, w_col, e_col = big[:, :n_cb], big[:, n_cb:n_cb + GROUP_CH], big[:, n_cb + GROUP_CH:]
            u8 = u_scr[ci]
            dt8 = dt_scr[ci]
            if conv_vals is None:
                cm = cc[pl.ds(r0, CHUNK), :]
            g = jnp.dot(cm, btc, preferred_element_type=F32)
            xs_b = xs.astype(BF16)
            w_parts, x_parts = [], []
            for r in range(nh):
                ef = cb[:, r * CHUNK:(r + 1) * CHUNK] - u8[r:r + 1, :]
                eb = u8[nh + r:nh + r + 1, :] - cb[:, (nh + r) * CHUNK:(nh + r + 1) * CHUNK]
                dt_f = dt8[r:r + 1, :]
                dt_b = dt8[nh + r:nh + r + 1, :]
                dt_sel = jnp.where(below, dt_f, jnp.where(above, dt_b, dt_f + dt_b))
                w_parts.append((g * (jnp.exp2(jnp.where(lower, ef, eb)) * dt_sel)).astype(BF16))
                x_parts.append(jnp.where(lane_head == r, xs_b, jnp.zeros_like(xs_b)))
            y = jnp.dot(jnp.concatenate(w_parts, axis=1), jnp.concatenate(x_parts, axis=0),
                        preferred_element_type=F32)
            y = y + e_col * jnp.dot(cm, st_prev.astype(BF16), preferred_element_type=F32)
            y_scr[pl.ds(y_off_row, CHUNK), :] = y + dskip * xs
        xw = (xs * w_col).astype(BF16)
        return st_prev * e_col[CHUNK - 1:CHUNK, :] + jnp.dot(btc, xw, preferred_element_type=F32)

    def bwd_chunk(ci, y_off_row, st_prev):
        r0 = pl.multiple_of(ci * CHUNK, CHUNK)
        xt = xt_scr[:, pl.ds(r0, CHUNK)]
        xs = xc[pl.ds(r0, CHUNK), :]
        btc = bt[:, pl.ds(r0, CHUNK)]
        big = jnp.dot(xt, rmat_b, preferred_element_type=F32)
        w_col, e_col = big[:, :GROUP_CH], big[:, GROUP_CH:]
        if y_off_row is not None:
            cm = cc[pl.ds(r0, CHUNK), :]
            y = y_scr[pl.ds(y_off_row, CHUNK), :]
            y = y + e_col * jnp.dot(cm, st_prev.astype(BF16), preferred_element_type=F32)
            gated = y * _silu(z_ref[pl.ds(y_off_row, CHUNK), :].astype(F32))
            ms = jnp.mean(gated * gated, axis=-1, keepdims=True)
            o_ref[pl.ds(y_off_row, CHUNK), :] = ((gated * lax.rsqrt(ms + EPS)) * ng_ref[...]).astype(o_ref.dtype)
        xw = (xs * w_col).astype(BF16)
        return st_prev * e_col[0:1, :] + jnp.dot(btc, xw, preferred_element_type=F32)

    unroll = math.gcd(SSD_UNROLL, n_lat_chunks)

    st = jnp.zeros((D_STATE, GROUP_CH), F32)
    for c in range(n_ctx_chunks):
        st = fwd_chunk(c, None, st)
    st_f[...] = st

    def fwd_body(i, carry):
        s = st_f[...]
        vals = conv_lat(i * unroll)
        for u in range(unroll):
            c = i * unroll + u
            nxt = conv_lat(c + 1) if u + 1 < unroll else None
            s = fwd_chunk(n_ctx_chunks + c, pl.multiple_of(c * CHUNK, CHUNK), s, vals)
            vals = nxt
        st_f[...] = s
        return carry

    lax.fori_loop(0, n_lat_chunks // unroll, fwd_body, 0)

    st = jnp.zeros((D_STATE, GROUP_CH), F32)
    for c in reversed(range(n_ctx_chunks)):
        st = bwd_chunk(c, None, st)
    st_b[...] = st

    def bwd_body(i, carry):
        s = st_b[...]
        for u in range(unroll):
            c = n_lat_chunks - 1 - (i * unroll + u)
            s = bwd_chunk(n_ctx_chunks + c, pl.multiple_of(c * CHUNK, CHUNK), s)
        st_b[...] = s
        return carry

    lax.fori_loop(0, n_lat_chunks // unroll, bwd_body, 0)


def _ssd(xbc_all, conv_w, conv_b, dt_c, aneg_col, dskip_row, z, norm_g, batch, seq, n_ctx):
    d_ssm = SSM_HEADS * SSM_HEAD_DIM
    b0 = d_ssm // D_STATE
    c0 = b0 + SSM_GROUPS
    ctx_blk0 = batch * seq // n_ctx
    n_ctx_chunks, n_lat_chunks = n_ctx // CHUNK, seq // CHUNK
    n_tok = n_ctx + seq
    in_specs = [
        pl.BlockSpec((seq, GROUP_CH), lambda b, g: (b, g)),
        pl.BlockSpec((seq, D_STATE), lambda b, g: (b, b0 + g)),
        pl.BlockSpec((seq, D_STATE), lambda b, g: (b, c0 + g)),
        pl.BlockSpec((n_ctx, GROUP_CH), lambda b, g: (ctx_blk0 + b, g)),
        pl.BlockSpec((n_ctx, D_STATE), lambda b, g: (ctx_blk0 + b, b0 + g)),
        pl.BlockSpec((n_ctx, D_STATE), lambda b, g: (ctx_blk0 + b, c0 + g)),
        pl.BlockSpec((CONV_W, GROUP_CH), lambda b, g: (0, g)),
        pl.BlockSpec((CONV_W, D_STATE), lambda b, g: (0, b0 + g)),
        pl.BlockSpec((CONV_W, D_STATE), lambda b, g: (0, c0 + g)),
        pl.BlockSpec((1, GROUP_CH), lambda b, g: (0, g)),
        pl.BlockSpec((1, D_STATE), lambda b, g: (0, b0 + g)),
        pl.BlockSpec((1, D_STATE), lambda b, g: (0, c0 + g)),
        pl.BlockSpec((n_lat_chunks, 8, CHUNK), lambda b, g: (b, g, 0)),
        pl.BlockSpec((n_ctx_chunks, 8, CHUNK), lambda b, g: (ctx_blk0 + b, g, 0)),
        pl.BlockSpec((8, 1), lambda b, g: (g, 0)),
        pl.BlockSpec((1, GROUP_CH), lambda b, g: (0, g)),
        pl.BlockSpec((seq, GROUP_CH), lambda b, g: (b, g)),
        pl.BlockSpec((1, GROUP_CH), lambda b, g: (0, g)),
    ]
    return pl.pallas_call(
        functools.partial(_ssd_kernel, seq=seq, n_ctx=n_ctx),
        grid=(batch, SSM_GROUPS),
        in_specs=in_specs,
        out_specs=pl.BlockSpec((seq, GROUP_CH), lambda b, g: (b, g)),
        out_shape=jax.ShapeDtypeStruct((batch * seq, d_ssm), BF16),
        scratch_shapes=[pltpu.VMEM((n_tok, GROUP_CH), F32),
                        pltpu.VMEM((D_STATE, n_tok), BF16),
                        pltpu.VMEM((n_tok, D_STATE), BF16),
                        pltpu.VMEM((seq, GROUP_CH), F32),
                        pltpu.VMEM((n_tok // CHUNK, 8, CHUNK), F32),
                        pltpu.VMEM((n_tok // CHUNK, 8, CHUNK), F32),
                        pltpu.VMEM((CHUNK, n_tok), BF16),
                        pltpu.VMEM((D_STATE, GROUP_CH), F32),
                        pltpu.VMEM((D_STATE, GROUP_CH), F32)],
        compiler_params=_cparams(2),
        name="ssd",
    )(xbc_all, xbc_all, xbc_all, xbc_all, xbc_all, xbc_all,
      conv_w, conv_w, conv_w, conv_b, conv_b, conv_b,
      dt_c, dt_c, aneg_col, dskip_row, z, norm_g)


def _rope_tables(seq):
    n_freq = HEAD_DIM // 4
    pos = np.arange(seq)
    inv = np.power(ROPE_THETA, -np.arange(n_freq, dtype=np.float64) / n_freq)
    ang_r = (pos // GRID_W)[:, None] * inv
    ang_c = (pos % GRID_W)[:, None] * inv
    cos_t = np.concatenate([np.cos(ang_r), np.cos(ang_r), np.cos(ang_c), np.cos(ang_c)], axis=1)
    sin_t = np.concatenate([-np.sin(ang_r), np.sin(ang_r), -np.sin(ang_c), np.sin(ang_c)], axis=1)
    return jnp.asarray(cos_t, F32), jnp.asarray(sin_t, F32)


def kernel(x, c, ctx, c_ctx, w_mod, b_mod, norm1, w_in, conv_w, conv_b, dt_bias, a_log, d_skip,
           ssm_norm, q_norm, k_norm, w_ssm_br, w_attn_br, w_o, norm2, w_ffn_in, w_ffn_out):
    batch, seq, d = x.shape
    n_ctx = ctx.shape[1]
    n_lat = batch * seq
    n_ctx_rows = batch * n_ctx
    n_all = n_lat + n_ctx_rows
    d_ssm = SSM_HEADS * SSM_HEAD_DIM
    d_conv = d_ssm + 2 * SSM_GROUPS * D_STATE
    d_attn = ATTN_HEADS * HEAD_DIM
    d_kv = KV_HEADS * HEAD_DIM
    d_ff = w_ffn_out.shape[1]
    n_dt = 2 * SSM_HEADS
    o_xbc = d_ssm
    o_dt = o_xbc + d_conv
    o_q = o_dt + n_dt
    o_k = o_q + d_attn
    o_g = o_k + 2 * d_kv
    assert w_mod.shape[0] == 1, "single-layer block"

    cvec = jnp.concatenate([c, c_ctx[None, :], jnp.zeros((8 - batch - 1, d), F32)], axis=0)
    mod = _modulation(cvec, w_mod[0], b_mod[0][None, :])
    mod3 = mod[:batch + 1].reshape(batch + 1, 1, N_MOD * d)

    w_in_t = jnp.transpose(w_in[0])
    perm = np.array([dr * SSM_HEADS + g * HEADS_PER_GROUP + r
                     for g in range(SSM_GROUPS) for dr in range(2) for r in range(HEADS_PER_GROUP)])
    w_dt_t = w_in_t[o_dt:o_dt + n_dt][perm].astype(BF16)
    dt_bias_col = dt_bias[0].reshape(n_dt)[perm][:, None]
    aneg_col = (-jnp.exp(a_log[0].astype(F32))).reshape(n_dt)[perm][:, None]

    x2d = x.reshape(n_lat, d)
    h_all, dt_c = _prenorm1(x2d, ctx.reshape(n_ctx_rows, d), norm1, mod3, batch, 0, 1, w_dt_t, dt_bias_col)

    tn = 1024
    tm_lat = 1024 if seq % 1024 == 0 else 512
    tm_all = n_all // 8 if (n_all // 8) % 16 == 0 and n_all % 8 == 0 else 512

    def in_proj(sections, rows, tm_rows, name):
        starts = np.cumsum([0] + [w // tn for _, w in sections])

        def row_off(j):
            off = sections[0][0] + j * tn
            for (first, _), s in zip(sections[1:], starts[1:]):
                off = jnp.where(j >= s, first + (j - s) * tn, off)
            return off

        return _matmul([(h_all, 0)], [(w_in_t, row_off, 0)], [], _ep_plain, int(starts[-1]) * tn, BF16,
                       m_rows=rows, tm=tm_rows, tn=tn, name=name, w_t=True)

    p_lat = in_proj([(0, d_ssm), (o_q, d_attn), (o_g, 2 * d)], n_lat, tm_lat, "proj_latent")
    p_all = in_proj([(o_xbc, d_conv), (o_k, 2 * d_kv)], n_all, tm_all, "proj_all")
    c_q = d_ssm
    c_g = d_ssm + d_attn
    c_k = d_conv
    c_v = d_conv + d_kv

    dskip_row = jnp.repeat(d_skip[0].astype(F32), SSM_HEAD_DIM)[None, :]
    y_norm = _ssd(p_all, conv_w[0], conv_b[0][None, :], dt_c, aneg_col, dskip_row, p_lat,
                  ssm_norm, batch, seq, n_ctx)

    cos_t, sin_t = _rope_tables(seq)
    attn = _attention(p_lat, c_q, p_all, c_k, c_v, q_norm, k_norm, cos_t, sin_t, batch, seq, n_ctx)

    tn_m = 512
    merged = _matmul([(y_norm, 0), (attn, 0)], [(w_ssm_br[0], 0, 0), (w_attn_br[0], 0, 1)],
                     [(p_lat, (tm_lat, tn_m), lambda j, m: (m, c_g // tn_m + j)),
                      (p_lat, (tm_lat, tn_m), lambda j, m: (m, (c_g + d) // tn_m + j))],
                     _ep_merge, d, BF16, m_rows=n_lat, tm=tm_lat, tn=tn_m, name="branch_merge")
    x_mid, h2 = _out_proj(merged, w_o[0], x2d, norm2, mod3, batch, 2, 3, 4)

    tn_f = 512
    act = _matmul([(h2, 0)], [(w_ffn_in[0], 0, 0), (w_ffn_in[0], d_ff // tn_f, 0)], [], _ep_swiglu,
                  d_ff, BF16, m_rows=n_lat, tm=tm_lat, tn=tn_f, name="ffn_in")
    tm_o = 512
    tiles_per_batch_o = seq // tm_o
    out = _matmul([(act, 0)], [(w_ffn_out[0], 0, 0)],
                  [(x_mid, (tm_o, tn_f), lambda j, m: (m, j)),
                   (mod3, (1, 1, tn_f), lambda j, m: (m // tiles_per_batch_o, 0, 5 * (d // tn_f) + j))],
                  _ep_gated_residual, d, F32, m_rows=n_lat, tm=tm_o, tn=tn_f, name="ffn_out")
    return out.reshape(batch, seq, d)
```

```python
import functools
import math

import jax
import jax.numpy as jnp
import numpy as np
from jax import lax
from jax.experimental import pallas as pl
from jax.experimental.pallas import tpu as pltpu

F32 = jnp.float32
BF16 = jnp.bfloat16

GRID_W = 64
SSM_HEADS = 32
SSM_HEAD_DIM = 64
SSM_GROUPS = 8
D_STATE = 128
CONV_W = 5
CHUNK = 128
ATTN_HEADS = 16
KV_HEADS = 4
HEAD_DIM = 128
ROPE_THETA = 10000.0
N_MOD = 6
EPS = 1e-6

HEADS_PER_GROUP = SSM_HEADS // SSM_GROUPS
GROUP_CH = HEADS_PER_GROUP * SSM_HEAD_DIM
Q_REP = ATTN_HEADS // KV_HEADS

VMEM_LIMIT_BYTES = 56 * 1024 * 1024
LOG2_E = 1.4426950408889634
SSD_UNROLL = 16


def _cparams(n_grid):
    return pltpu.CompilerParams(dimension_semantics=("arbitrary",) * n_grid,
                                vmem_limit_bytes=VMEM_LIMIT_BYTES)


def _silu(v):
    return v * jax.nn.sigmoid(v)


def _mod_kernel(c_ref, w_ref, b_ref, o_ref):
    s = _silu(c_ref[...]).astype(BF16)
    o_ref[...] = jnp.dot(s, w_ref[...].astype(BF16), preferred_element_type=F32) + b_ref[...]


def _modulation(cvec, w_mod, b_mod):
    rows, d = cvec.shape
    n = w_mod.shape[1]
    tn = 1024
    return pl.pallas_call(
        _mod_kernel,
        grid=(n // tn,),
        in_specs=[pl.BlockSpec((rows, d), lambda j: (0, 0)),
                  pl.BlockSpec((d, tn), lambda j: (0, j)),
                  pl.BlockSpec((1, tn), lambda j: (0, j))],
        out_specs=pl.BlockSpec((rows, tn), lambda j: (0, j)),
        out_shape=jax.ShapeDtypeStruct((rows, n), F32),
        compiler_params=_cparams(1),
        name="modulation",
    )(cvec, w_mod, b_mod)


PRENORM_ROWS = 512


NORM_ROWS = 16


def _norm_mod(t, g, sh, sc):
    ms = jnp.mean(t * t, axis=-1, keepdims=True)
    return ((t * lax.rsqrt(ms + EPS)) * g) * (1.0 + sc) + sh


def _norm_mod_rows(src_ref, dst_ref, g_ref, sh_ref, sc_ref):
    g = g_ref[...]
    sh = sh_ref[0]
    sc = sc_ref[0]

    def body(k, carry):
        rows = pl.ds(pl.multiple_of(k * NORM_ROWS, NORM_ROWS), NORM_ROWS)
        dst_ref[rows, :] = _norm_mod(src_ref[rows, :], g, sh, sc).astype(dst_ref.dtype)
        return carry

    lax.fori_loop(0, src_ref.shape[0] // NORM_ROWS, body, 0, unroll=8)


def _prenorm1_kernel(x_ref, ctx_ref, g_ref, sh_ref, sc_ref, wdt_ref, bdt_ref, o_ref, dt_ref, *, n_lat_tiles):
    i = pl.program_id(0)

    @pl.when(i < n_lat_tiles)
    def _():
        _norm_mod_rows(x_ref, o_ref, g_ref, sh_ref, sc_ref)

    @pl.when(i >= n_lat_tiles)
    def _():
        _norm_mod_rows(ctx_ref, o_ref, g_ref, sh_ref, sc_ref)

    raw = lax.dot_general(wdt_ref[...], o_ref[...], (((1,), (1,)), ((), ())),
                          preferred_element_type=F32) + bdt_ref[...]
    dt = jnp.maximum(raw, 0.0) + jnp.log1p(jnp.exp(-jnp.abs(raw)))
    for c in range(dt_ref.shape[0]):
        dt_ref[c] = dt[:, c * CHUNK:(c + 1) * CHUNK]


def _prenorm1(x2d, ctx2d, norm_g, mod3, batch, shift_blk, scale_blk, w_dt_t, dt_bias_col):
    n_lat, d = x2d.shape
    n_ctx = ctx2d.shape[0]
    n_dt = w_dt_t.shape[0]
    tr = math.gcd(PRENORM_ROWS, math.gcd(n_lat // batch, n_ctx))
    lat_tiles, ctx_tiles = n_lat // tr, n_ctx // tr
    tiles_per_batch = lat_tiles // batch

    def mod_row(i):
        return jnp.where(i < lat_tiles, i // tiles_per_batch, batch)

    return pl.pallas_call(
        functools.partial(_prenorm1_kernel, n_lat_tiles=lat_tiles),
        grid=(lat_tiles + ctx_tiles,),
        in_specs=[pl.BlockSpec((tr, d), lambda i: (jnp.minimum(i, lat_tiles - 1), 0)),
                  pl.BlockSpec((tr, d), lambda i: (jnp.maximum(i - lat_tiles, 0), 0)),
                  pl.BlockSpec((1, d), lambda i: (0, 0)),
                  pl.BlockSpec((1, 1, d), lambda i: (mod_row(i), 0, shift_blk)),
                  pl.BlockSpec((1, 1, d), lambda i: (mod_row(i), 0, scale_blk)),
                  pl.BlockSpec((n_dt, d), lambda i: (0, 0)),
                  pl.BlockSpec((n_dt, 1), lambda i: (0, 0))],
        out_specs=[pl.BlockSpec((tr, d), lambda i: (i, 0)),
                   pl.BlockSpec((tr // CHUNK, n_dt, CHUNK), lambda i: (i, 0, 0))],
        out_shape=[jax.ShapeDtypeStruct((n_lat + n_ctx, d), BF16),
                   jax.ShapeDtypeStruct(((n_lat + n_ctx) // CHUNK, n_dt, CHUNK), F32)],
        compiler_params=_cparams(1),
        name="prenorm1",
    )(x2d, ctx2d, norm_g, mod3, mod3, w_dt_t, dt_bias_col)


OUT_PROJ_ROWS = 512
OUT_PROJ_SECTIONS = 2


def _out_proj_kernel(a_ref, w_ref, x_ref, gt_ref, g_ref, sh_ref, sc_ref, xmid_ref, h_ref, w_scr):
    @pl.when(pl.program_id(0) == 0)
    def _():
        w_scr[...] = w_ref[...].astype(BF16)

    half = a_ref.shape[0] // OUT_PROJ_SECTIONS
    g, sh, sc, gt = g_ref[...], sh_ref[0], sc_ref[0], gt_ref[0]
    for h0 in range(0, a_ref.shape[0], half):
        xmid_ref[h0:h0 + half, :] = x_ref[h0:h0 + half, :] + gt * jnp.dot(
            a_ref[h0:h0 + half, :], w_scr[...], preferred_element_type=F32)
        for r in range(h0, h0 + half, NORM_ROWS):
            h_ref[r:r + NORM_ROWS, :] = _norm_mod(xmid_ref[r:r + NORM_ROWS, :], g, sh, sc).astype(h_ref.dtype)


def _out_proj(a, w, x2d, norm_g, mod3, batch, gate_blk, shift_blk, scale_blk):
    n_lat, d = x2d.shape
    tr = math.gcd(OUT_PROJ_ROWS, n_lat // batch)
    tiles_per_batch = n_lat // tr // batch

    def mod_spec(blk):
        return pl.BlockSpec((1, 1, d), lambda i: (i // tiles_per_batch, 0, blk))

    return pl.pallas_call(
        _out_proj_kernel,
        grid=(n_lat // tr,),
        in_specs=[pl.BlockSpec((tr, a.shape[1]), lambda i: (i, 0)),
                  pl.BlockSpec(w.shape, lambda i: (0, 0), pipeline_mode=pl.Buffered(1)),
                  pl.BlockSpec((tr, d), lambda i: (i, 0)),
                  mod_spec(gate_blk),
                  pl.BlockSpec((1, d), lambda i: (0, 0)),
                  mod_spec(shift_blk),
                  mod_spec(scale_blk)],
        out_specs=[pl.BlockSpec((tr, d), lambda i: (i, 0)),
                   pl.BlockSpec((tr, d), lambda i: (i, 0))],
        out_shape=[jax.ShapeDtypeStruct((n_lat, d), F32),
                   jax.ShapeDtypeStruct((n_lat, d), BF16)],
        scratch_shapes=[pltpu.VMEM(w.shape, BF16)],
        compiler_params=_cparams(1),
        name="out_proj",
    )(a, w, x2d, mod3, norm_g, mod3, mod3)


def _mm_kernel(*refs, n_a, which_a, n_extra, epilogue, w_t):
    n_w = len(which_a)
    a_refs = refs[:n_a]
    w_refs = refs[n_a:n_a + n_w]
    e_refs = refs[n_a + n_w:n_a + n_w + n_extra]
    o_ref = refs[n_a + n_w + n_extra]
    w_scr = refs[n_a + n_w + n_extra + 1:]

    @pl.when(pl.program_id(1) == 0)
    def _():
        for k in range(n_w):
            w_scr[k][...] = w_refs[k][...].astype(BF16)

    dims = (((1,), (1,)), ((), ())) if w_t else (((1,), (0,)), ((), ()))
    accs = [lax.dot_general(a_refs[which_a[k]][...], w_scr[k][...], dims, preferred_element_type=F32)
            for k in range(n_w)]
    o_ref[...] = epilogue(accs, e_refs).astype(o_ref.dtype)


def _matmul(a_list, w_list, extras, epilogue, n_out, out_dtype, *, m_rows, tm, tn, name, w_t=False):
    grid = (n_out // tn, m_rows // tm)
    in_specs, args, scratch = [], [], []
    for a, off in a_list:
        in_specs.append(pl.BlockSpec((tm, a.shape[1]), lambda j, m, off=off: (m + off, 0)))
        args.append(a)
    for w, off, _ in w_list:
        if w_t:
            in_specs.append(pl.BlockSpec((pl.Element(tn), pl.Element(w.shape[1])),
                                         lambda j, m, off=off: (pl.multiple_of(off(j), 8), 0)))
            scratch.append(pltpu.VMEM((tn, w.shape[1]), BF16))
        else:
            in_specs.append(pl.BlockSpec((w.shape[0], tn), lambda j, m, off=off: (0, j + off)))
            scratch.append(pltpu.VMEM((w.shape[0], tn), BF16))
        args.append(w)
    for e, blk, imap in extras:
        in_specs.append(pl.BlockSpec(blk, imap))
        args.append(e)
    kern = functools.partial(_mm_kernel, n_a=len(a_list), which_a=tuple(w[2] for w in w_list),
                             n_extra=len(extras), epilogue=epilogue, w_t=w_t)
    return pl.pallas_call(
        kern,
        grid=grid,
        in_specs=in_specs,
        out_specs=pl.BlockSpec((tm, tn), lambda j, m: (m, j)),
        out_shape=jax.ShapeDtypeStruct((m_rows, n_out), out_dtype),
        scratch_shapes=scratch,
        compiler_params=_cparams(2),
        name=name,
    )(*args)


def _ep_plain(accs, e_refs):
    return accs[0]


def _ep_swiglu(accs, e_refs):
    return _silu(accs[0]) * accs[1]


def _ep_merge(accs, e_refs):
    g_ssm = e_refs[0][...].astype(F32)
    g_attn = e_refs[1][...].astype(F32)
    return jax.nn.sigmoid(g_ssm) * accs[0] + jax.nn.sigmoid(g_attn) * accs[1]


def _ep_gated_residual(accs, e_refs):
    return e_refs[0][...] + e_refs[1][0] * accs[0]


def _norm_rope(t, g, cs, sn, scale):
    tn = (t * lax.rsqrt(jnp.mean(t * t, axis=-1, keepdims=True) + EPS)) * g
    lane = lax.broadcasted_iota(jnp.int32, tn.shape, 1)
    first_half = (lane % (HEAD_DIM // 2)) < (HEAD_DIM // 4)
    rot = jnp.where(first_half,
                    pltpu.roll(tn, HEAD_DIM - HEAD_DIM // 4, 1),
                    pltpu.roll(tn, HEAD_DIM // 4, 1))
    return (tn * cs + rot * sn) * scale


def _attn_kernel(q_ref, kl_ref, kc_ref, vl_ref, vc_ref, qg_ref, kg_ref, cos_ref, sin_ref, o_ref,
                 k_scr, v_scr, s0_scr, s1_scr, p0_scr, p1_scr, *, n_ctx, tq, q_scale):
    seq = q_ref.shape[0]
    n_tiles = seq // tq
    kg = kg_ref[...]
    qg = qg_ref[...]

    kc = kc_ref[...].astype(F32)
    k_scr[0:n_ctx, :] = ((kc * lax.rsqrt(jnp.mean(kc * kc, axis=-1, keepdims=True) + EPS)) * kg).astype(BF16)

    def k_body(i, carry):
        r0 = pl.multiple_of(i * tq, tq)
        kt = _norm_rope(kl_ref[pl.ds(r0, tq), :].astype(F32), kg,
                        cos_ref[pl.ds(r0, tq), :], sin_ref[pl.ds(r0, tq), :], 1.0)
        k_scr[pl.ds(pl.multiple_of(n_ctx + r0, tq), tq), :] = kt.astype(BF16)
        return carry

    lax.fori_loop(0, n_tiles, k_body, 0, unroll=2)
    v_scr[0:n_ctx, 0:HEAD_DIM] = vc_ref[...]
    v_scr[n_ctx:, 0:HEAD_DIM] = vl_ref[...]
    v_scr[:, HEAD_DIM:] = jnp.ones((v_scr.shape[0], HEAD_DIM), BF16)

    s_bufs = (s0_scr, s1_scr)
    p_bufs = (p0_scr, p1_scr)

    def scores(t, r):
        r0 = pl.multiple_of(t * tq, tq)
        q = _norm_rope(q_ref[pl.ds(r0, tq), r * HEAD_DIM:(r + 1) * HEAD_DIM].astype(F32), qg,
                       cos_ref[pl.ds(r0, tq), :], sin_ref[pl.ds(r0, tq), :], q_scale).astype(BF16)
        s_bufs[r % 2][...] = lax.dot_general(q, k_scr[...], (((1,), (1,)), ((), ())),
                                             preferred_element_type=F32)

    def softmax(r):
        sc = s_bufs[r % 2][...]
        mx = jnp.max(sc, axis=-1, keepdims=True)
        p_bufs[r % 2][...] = jnp.exp2((sc - mx).astype(BF16))

    def values(t, r):
        r0 = pl.multiple_of(t * tq, tq)
        o = jnp.dot(p_bufs[r % 2][...], v_scr[...], preferred_element_type=F32)
        den = o[:, HEAD_DIM:HEAD_DIM + 1]
        o_ref[pl.ds(r0, tq), r * HEAD_DIM:(r + 1) * HEAD_DIM] = (o[:, :HEAD_DIM] / den).astype(o_ref.dtype)

    def tile(t, first):
        for r in range(Q_REP):
            if r >= 1:
                softmax(r - 1)
            elif not first:
                softmax(Q_REP - 1)
            if r >= 2:
                values(t, r - 2)
            elif not first:
                values(t - 1, Q_REP - 2 + r)
            scores(t, r)

    tile(0, True)

    def body(t, carry):
        tile(t, False)
        return carry

    lax.fori_loop(1, n_tiles, body, 0)
    softmax(Q_REP - 1)
    values(n_tiles - 1, Q_REP - 2)
    values(n_tiles - 1, Q_REP - 1)


def _attention(p_lat, c_q, p_all, c_k, c_v, q_norm, k_norm, cos_t, sin_t, batch, seq, n_ctx):
    tq = 256
    assert Q_REP % 2 == 0 and Q_REP >= 2
    ctx_blk0 = batch * seq // n_ctx
    gw = Q_REP * HEAD_DIM
    n_keys = n_ctx + seq
    qb, kb, vb = c_q // gw, c_k // HEAD_DIM, c_v // HEAD_DIM
    const = dict(pipeline_mode=pl.Buffered(1))
    return pl.pallas_call(
        functools.partial(_attn_kernel, n_ctx=n_ctx, tq=tq, q_scale=HEAD_DIM ** -0.5 * LOG2_E),
        grid=(batch, KV_HEADS),
        in_specs=[pl.BlockSpec((seq, gw), lambda b, g: (b, qb + g)),
                  pl.BlockSpec((seq, HEAD_DIM), lambda b, g: (b, kb + g)),
                  pl.BlockSpec((n_ctx, HEAD_DIM), lambda b, g: (ctx_blk0 + b, kb + g)),
                  pl.BlockSpec((seq, HEAD_DIM), lambda b, g: (b, vb + g)),
                  pl.BlockSpec((n_ctx, HEAD_DIM), lambda b, g: (ctx_blk0 + b, vb + g)),
                  pl.BlockSpec((1, HEAD_DIM), lambda b, g: (0, 0)),
                  pl.BlockSpec((1, HEAD_DIM), lambda b, g: (0, 0)),
                  pl.BlockSpec((seq, HEAD_DIM), lambda b, g: (0, 0), **const),
                  pl.BlockSpec((seq, HEAD_DIM), lambda b, g: (0, 0), **const)],
        out_specs=pl.BlockSpec((seq, gw), lambda b, g: (b, g)),
        out_shape=jax.ShapeDtypeStruct((batch * seq, ATTN_HEADS * HEAD_DIM), BF16),
        scratch_shapes=[pltpu.VMEM((n_keys, HEAD_DIM), BF16),
                        pltpu.VMEM((n_keys, 2 * HEAD_DIM), BF16),
                        pltpu.VMEM((tq, n_keys), F32),
                        pltpu.VMEM((tq, n_keys), F32),
                        pltpu.VMEM((tq, n_keys), BF16),
                        pltpu.VMEM((tq, n_keys), BF16)],
        compiler_params=_cparams(2),
        name="attention",
    )(p_lat, p_all, p_all, p_all, p_all, q_norm, k_norm, cos_t, sin_t)


def _split3(v):
    hi = v.astype(BF16).astype(F32)
    r1 = v - hi
    mid = r1.astype(BF16).astype(F32)
    lo = (r1 - mid).astype(BF16).astype(F32)
    return hi, mid, lo


CONV_HALO = 16


def _conv_shift_matrix():
    pad = (CONV_W - 1) // 2
    taps = [j for j in range(CONV_W) if j != pad]
    rows = lax.broadcasted_iota(jnp.int32, (len(taps) * CHUNK, CHUNK + 2 * CONV_HALO), 0)
    cols = lax.broadcasted_iota(jnp.int32, (len(taps) * CHUNK, CHUNK + 2 * CONV_HALO), 1)
    tap = jnp.where(rows // CHUNK < pad, rows // CHUNK, rows // CHUNK + 1)
    return (cols == CONV_HALO - pad + tap + rows % CHUNK).astype(BF16), taps


def _conv_block_fn(streams, n_rows, shift, taps):
    n_blk = n_rows // CHUNK
    pad = (CONV_W - 1) // 2

    def lanes(parts):
        return parts[0] if len(parts) == 1 else jnp.concatenate(parts, axis=1)

    def block(i):
        r0 = pl.multiple_of(i * CHUNK, CHUNK)
        lo = pl.multiple_of(jnp.maximum(r0 - CONV_HALO, 0), CONV_HALO)
        hi = pl.multiple_of(jnp.minimum(r0 + CHUNK, n_rows - CONV_HALO), CONV_HALO)
        out = []
        for src_refs, w_refs, b_refs, store in streams:
            cur = lanes([s[pl.ds(r0, CHUNK), :] for s in src_refs])
            prev = lanes([s[pl.ds(lo, CONV_HALO), :] for s in src_refs])
            nxt = lanes([s[pl.ds(hi, CONV_HALO), :] for s in src_refs])
            prev = jnp.where(i > 0, prev, jnp.zeros_like(prev))
            nxt = jnp.where(i < n_blk - 1, nxt, jnp.zeros_like(nxt))
            window = jnp.concatenate([prev, cur, nxt], axis=0)
            shifted = jnp.dot(shift, window, preferred_element_type=F32)
            w = lanes([r[...] for r in w_refs])
            acc = lanes([r[...] for r in b_refs]) + w[pad:pad + 1, :] * cur.astype(F32)
            for k, j in enumerate(taps):
                acc = acc + w[j:j + 1, :] * shifted[k * CHUNK:(k + 1) * CHUNK, :]
            out.append(store(r0, _silu(acc)))
        return out

    return block


def _ssd_kernel(x_lat, b_lat, c_lat, x_ctx, b_ctx, c_ctx,
                cwx, cwb, cwc, cbx, cbb, cbc,
                dt_lat, dt_ctx, aneg_ref, dskip_ref, z_ref, ng_ref,
                o_ref,
                xc, bt, cc, y_scr, dt_scr, u_scr, xt_scr, st_f, st_b, *, seq, n_ctx):
    nh = HEADS_PER_GROUP
    n_ctx_chunks = n_ctx // CHUNK
    n_lat_chunks = seq // CHUNK
    n_chunks = n_ctx_chunks + n_lat_chunks

    def store_x(off):
        def f(r0, v):
            xc[pl.ds(pl.multiple_of(off + r0, CHUNK), CHUNK), :] = v
            return v
        return f

    def store_bc(off):
        def f(r0, v):
            r = pl.multiple_of(off + r0, CHUNK)
            b_t = v[:, :D_STATE].T.astype(BF16)
            c_m = v[:, D_STATE:].astype(BF16)
            bt[:, pl.ds(r, CHUNK)] = b_t
            cc[pl.ds(r, CHUNK), :] = c_m
            return b_t, c_m
        return f

    shift, taps = _conv_shift_matrix()
    conv_ctx = _conv_block_fn([([x_ctx], [cwx], [cbx], store_x(0)),
                               ([b_ctx, c_ctx], [cwb, cwc], [cbb, cbc], store_bc(0))], n_ctx, shift, taps)
    conv_lat = _conv_block_fn([([x_lat], [cwx], [cbx], store_x(n_ctx)),
                               ([b_lat, c_lat], [cwb, cwc], [cbb, cbc], store_bc(n_ctx))], seq, shift, taps)
    for c in range(n_ctx_chunks):
        conv_ctx(c)

    row = lax.broadcasted_iota(jnp.int32, (CHUNK, CHUNK), 0)
    col = lax.broadcasted_iota(jnp.int32, (CHUNK, CHUNK), 1)
    triu = (row <= col).astype(BF16)
    eye = (row == col).astype(BF16)
    lower = row >= col
    below = row > col
    above = row < col
    dskip = dskip_ref[...]
    lane_head = lax.broadcasted_iota(jnp.int32, (CHUNK, GROUP_CH), 1) // SSM_HEAD_DIM

    def spread(n_cols, q_of_col, hd_of_col):
        k = lax.broadcasted_iota(jnp.int32, (CHUNK, n_cols), 0)
        c = lax.broadcasted_iota(jnp.int32, (CHUNK, n_cols), 1)
        return ((k < 72) & (k // 24 == q_of_col(c)) & (k % 8 == hd_of_col(c))).astype(BF16)

    n_cb = 2 * nh * CHUNK
    rmat_f = spread(n_cb + 2 * GROUP_CH,
                    lambda c: jnp.where(c < n_cb, 0, jnp.where(c < n_cb + GROUP_CH, 1, 2)),
                    lambda c: jnp.where(c < n_cb, c // CHUNK, ((c - n_cb) % GROUP_CH) // SSM_HEAD_DIM))
    rmat_b = spread(2 * GROUP_CH,
                    lambda c: jnp.where(c < GROUP_CH, 1, 2),
                    lambda c: nh + (c % GROUP_CH) // SSM_HEAD_DIM)

    dt_scr[0:n_ctx_chunks] = dt_ctx[...]
    dt_scr[n_ctx_chunks:n_chunks] = dt_lat[...]
    dt3 = dt_scr[...]
    n_rows = n_chunks * 8
    dt2 = dt3.reshape(n_rows, CHUNK)
    a2 = (dt3 * aneg_ref[...][None]).reshape(n_rows, CHUNK)
    cum = sum(jnp.dot(part.astype(BF16), triu, preferred_element_type=F32) for part in _split3(a2))
    total = jnp.broadcast_to(cum[:, CHUNK - 1:CHUNK], cum.shape)
    cumx = cum - a2
    is_fwd = (lax.broadcasted_iota(jnp.int32, (n_rows, CHUNK), 0) % 8) < nh
    u2 = jnp.where(is_fwd, cum, cumx) * LOG2_E
    w2 = dt2 * jnp.exp(jnp.where(is_fwd, total - cum, cumx))
    e2 = jnp.exp(jnp.where(is_fwd, cum, total - cumx))
    u_scr[...] = u2.reshape(n_chunks, 8, CHUNK)
    pieces = [p.reshape(n_chunks, 8, CHUNK) for v in (u2, w2, e2) for p in _split3(v)]
    pieces.append(jnp.zeros((n_chunks, CHUNK - 8 * len(pieces), CHUNK), F32))
    packed = jnp.concatenate(pieces, axis=1).reshape(n_chunks * CHUNK, CHUNK).astype(BF16)
    xt_scr[...] = lax.dot_general(eye, packed, (((1,), (1,)), ((), ())),
                                  preferred_element_type=F32).astype(BF16)

    def finish(y_off_row, y):
        y = y + y_scr[pl.ds(y_off_row, CHUNK), :]
        gated = y * _silu(z_ref[pl.ds(y_off_row, CHUNK), :].astype(F32))
        ms = jnp.mean(gated * gated, axis=-1, keepdims=True)
        o_ref[pl.ds(y_off_row, CHUNK), :] = ((gated * lax.rsqrt(ms + EPS)) * ng_ref[...]).astype(o_ref.dtype)

    def emit(y_off_row, y, last):
        if last:
            finish(y_off_row, y)
        else:
            y_scr[pl.ds(y_off_row, CHUNK), :] = y

    def fwd_chunk(ci, y_off_row, st_prev, conv_vals=None, last=False):
        r0 = pl.multiple_of(ci * CHUNK, CHUNK)
        xt = xt_scr[:, pl.ds(r0, CHUNK)]
        if conv_vals is None:
            xs = xc[pl.ds(r0, CHUNK), :]
            btc = bt[:, pl.ds(r0, CHUNK)]
        else:
            xs, (btc, cm) = conv_vals
        if y_off_row is None:
            big = jnp.dot(xt, rmat_f[:, n_cb:], preferred_element_type=F32)
            w_col, e_col = big[:, :GROUP_CH], big[:, GROUP_CH:]
        else:
            big = jnp.dot(xt, rmat_f, preferred_element_type=F32)
            cb, w_col, e_col = big[:, :n_cb], big[:, n_cb:n_cb + GROUP_CH], big[:, n_cb + GROUP_CH:]
            u8 = u_scr[ci]
            dt8 = dt_scr[ci]
            if conv_vals is None:
                cm = cc[pl.ds(r0, CHUNK), :]
            g = jnp.dot(cm, btc, preferred_element_type=F32)
            xs_b = xs.astype(BF16)
            w_parts, x_parts = [], []
            for r in range(nh):
                ef = cb[:, r * CHUNK:(r + 1) * CHUNK] - u8[r:r + 1, :]
                eb = u8[nh + r:nh + r + 1, :] - cb[:, (nh + r) * CHUNK:(nh + r + 1) * CHUNK]
                dt_f = dt8[r:r + 1, :]
                dt_b = dt8[nh + r:nh + r + 1, :]
                dt_sel = jnp.where(below, dt_f, jnp.where(above, dt_b, dt_f + dt_b))
                w_parts.append((g * (jnp.exp2(jnp.where(lower, ef, eb)) * dt_sel)).astype(BF16))
                x_parts.append(jnp.where(lane_head == r, xs_b, jnp.zeros_like(xs_b)))
            y = jnp.dot(jnp.concatenate(w_parts, axis=1), jnp.concatenate(x_parts, axis=0),
                        preferred_element_type=F32)
            y = y + e_col * jnp.dot(cm, st_prev.astype(BF16), preferred_element_type=F32)
            emit(y_off_row, y + dskip * xs, last)
        xw = (xs * w_col).astype(BF16)
        return st_prev * e_col[CHUNK - 1:CHUNK, :] + jnp.dot(btc, xw, preferred_element_type=F32)

    def bwd_chunk(ci, y_off_row, st_prev, conv_vals=None, last=False):
        r0 = pl.multiple_of(ci * CHUNK, CHUNK)
        xt = xt_scr[:, pl.ds(r0, CHUNK)]
        if conv_vals is None:
            xs = xc[pl.ds(r0, CHUNK), :]
            btc = bt[:, pl.ds(r0, CHUNK)]
        else:
            xs, (btc, cm) = conv_vals
        big = jnp.dot(xt, rmat_b, preferred_element_type=F32)
        w_col, e_col = big[:, :GROUP_CH], big[:, GROUP_CH:]
        if y_off_row is not None:
            if conv_vals is None:
                cm = cc[pl.ds(r0, CHUNK), :]
            emit(y_off_row, e_col * jnp.dot(cm, st_prev.astype(BF16), preferred_element_type=F32), last)
        xw = (xs * w_col).astype(BF16)
        return st_prev * e_col[0:1, :] + jnp.dot(btc, xw, preferred_element_type=F32)

    st = jnp.zeros((D_STATE, GROUP_CH), F32)
    for c in range(n_ctx_chunks):
        st = fwd_chunk(c, None, st)
    st_f[...] = st
    st = jnp.zeros((D_STATE, GROUP_CH), F32)
    for c in reversed(range(n_ctx_chunks)):
        st = bwd_chunk(c, None, st)
    st_b[...] = st

    assert n_lat_chunks % 2 == 0
    half = n_lat_chunks // 2
    unroll = math.gcd(SSD_UNROLL, half)

    def row_of(c):
        return pl.multiple_of(c * CHUNK, CHUNK)

    def first_half(i, carry):
        sf, sb = st_f[...], st_b[...]
        vf = conv_lat(i * unroll)
        vb = conv_lat(n_lat_chunks - 1 - i * unroll)
        for u in range(unroll):
            kf = i * unroll + u
            kb = n_lat_chunks - 1 - kf
            nf = conv_lat(kf + 1) if u + 1 < unroll else None
            nb = conv_lat(kb - 1) if u + 1 < unroll else None
            sf = fwd_chunk(n_ctx_chunks + kf, row_of(kf), sf, vf)
            sb = bwd_chunk(n_ctx_chunks + kb, row_of(kb), sb, vb)
            vf, vb = nf, nb
        st_f[...] = sf
        st_b[...] = sb
        return carry

    lax.fori_loop(0, half // unroll, first_half, 0)

    def second_half(i, carry):
        sf, sb = st_f[...], st_b[...]
        for u in range(unroll):
            kf = half + i * unroll + u
            kb = n_lat_chunks - 1 - kf
            sf = fwd_chunk(n_ctx_chunks + kf, row_of(kf), sf, last=True)
            sb = bwd_chunk(n_ctx_chunks + kb, row_of(kb), sb, last=True)
        st_f[...] = sf
        st_b[...] = sb
        return carry

    lax.fori_loop(0, half // unroll, second_half, 0)


def _ssd(xbc_all, conv_w, conv_b, dt_c, aneg_col, dskip_row, z, norm_g, batch, seq, n_ctx):
    d_ssm = SSM_HEADS * SSM_HEAD_DIM
    b0 = d_ssm // D_STATE
    c0 = b0 + SSM_GROUPS
    ctx_blk0 = batch * seq // n_ctx
    n_ctx_chunks, n_lat_chunks = n_ctx // CHUNK, seq // CHUNK
    n_tok = n_ctx + seq
    in_specs = [
        pl.BlockSpec((seq, GROUP_CH), lambda b, g: (b, g)),
        pl.BlockSpec((seq, D_STATE), lambda b, g: (b, b0 + g)),
        pl.BlockSpec((seq, D_STATE), lambda b, g: (b, c0 + g)),
        pl.BlockSpec((n_ctx, GROUP_CH), lambda b, g: (ctx_blk0 + b, g)),
        pl.BlockSpec((n_ctx, D_STATE), lambda b, g: (ctx_blk0 + b, b0 + g)),
        pl.BlockSpec((n_ctx, D_STATE), lambda b, g: (ctx_blk0 + b, c0 + g)),
        pl.BlockSpec((CONV_W, GROUP_CH), lambda b, g: (0, g)),
        pl.BlockSpec((CONV_W, D_STATE), lambda b, g: (0, b0 + g)),
        pl.BlockSpec((CONV_W, D_STATE), lambda b, g: (0, c0 + g)),
        pl.BlockSpec((1, GROUP_CH), lambda b, g: (0, g)),
        pl.BlockSpec((1, D_STATE), lambda b, g: (0, b0 + g)),
        pl.BlockSpec((1, D_STATE), lambda b, g: (0, c0 + g)),
        pl.BlockSpec((n_lat_chunks, 8, CHUNK), lambda b, g: (b, g, 0)),
        pl.BlockSpec((n_ctx_chunks, 8, CHUNK), lambda b, g: (ctx_blk0 + b, g, 0)),
        pl.BlockSpec((8, 1), lambda b, g: (g, 0)),
        pl.BlockSpec((1, GROUP_CH), lambda b, g: (0, g)),
        pl.BlockSpec((seq, GROUP_CH), lambda b, g: (b, g)),
        pl.BlockSpec((1, GROUP_CH), lambda b, g: (0, g)),
    ]
    return pl.pallas_call(
        functools.partial(_ssd_kernel, seq=seq, n_ctx=n_ctx),
        grid=(batch, SSM_GROUPS),
        in_specs=in_specs,
        out_specs=pl.BlockSpec((seq, GROUP_CH), lambda b, g: (b, g)),
        out_shape=jax.ShapeDtypeStruct((batch * seq, d_ssm), BF16),
        scratch_shapes=[pltpu.VMEM((n_tok, GROUP_CH), F32),
                        pltpu.VMEM((D_STATE, n_tok), BF16),
                        pltpu.VMEM((n_tok, D_STATE), BF16),
                        pltpu.VMEM((seq, GROUP_CH), F32),
                        pltpu.VMEM((n_tok // CHUNK, 8, CHUNK), F32),
                        pltpu.VMEM((n_tok // CHUNK, 8, CHUNK), F32),
                        pltpu.VMEM((CHUNK, n_tok), BF16),
                        pltpu.VMEM((D_STATE, GROUP_CH), F32),
                        pltpu.VMEM((D_STATE, GROUP_CH), F32)],
        compiler_params=_cparams(2),
        name="ssd",
    )(xbc_all, xbc_all, xbc_all, xbc_all, xbc_all, xbc_all,
      conv_w, conv_w, conv_w, conv_b, conv_b, conv_b,
      dt_c, dt_c, aneg_col, dskip_row, z, norm_g)


def _rope_tables(seq):
    n_freq = HEAD_DIM // 4
    pos = np.arange(seq)
    inv = np.power(ROPE_THETA, -np.arange(n_freq, dtype=np.float64) / n_freq)
    ang_r = (pos // GRID_W)[:, None] * inv
    ang_c = (pos % GRID_W)[:, None] * inv
    cos_t = np.concatenate([np.cos(ang_r), np.cos(ang_r), np.cos(ang_c), np.cos(ang_c)], axis=1)
    sin_t = np.concatenate([-np.sin(ang_r), np.sin(ang_r), -np.sin(ang_c), np.sin(ang_c)], axis=1)
    return jnp.asarray(cos_t, F32), jnp.asarray(sin_t, F32)


def kernel(x, c, ctx, c_ctx, w_mod, b_mod, norm1, w_in, conv_w, conv_b, dt_bias, a_log, d_skip,
           ssm_norm, q_norm, k_norm, w_ssm_br, w_attn_br, w_o, norm2, w_ffn_in, w_ffn_out):
    batch, seq, d = x.shape
    n_ctx = ctx.shape[1]
    n_lat = batch * seq
    n_ctx_rows = batch * n_ctx
    n_all = n_lat + n_ctx_rows
    d_ssm = SSM_HEADS * SSM_HEAD_DIM
    d_conv = d_ssm + 2 * SSM_GROUPS * D_STATE
    d_attn = ATTN_HEADS * HEAD_DIM
    d_kv = KV_HEADS * HEAD_DIM
    d_ff = w_ffn_out.shape[1]
    n_dt = 2 * SSM_HEADS
    o_xbc = d_ssm
    o_dt = o_xbc + d_conv
    o_q = o_dt + n_dt
    o_k = o_q + d_attn
    o_g = o_k + 2 * d_kv
    assert w_mod.shape[0] == 1, "single-layer block"

    cvec = jnp.concatenate([c, c_ctx[None, :], jnp.zeros((8 - batch - 1, d), F32)], axis=0)
    mod = _modulation(cvec, w_mod[0], b_mod[0][None, :])
    mod3 = mod[:batch + 1].reshape(batch + 1, 1, N_MOD * d)

    w_in_t = jnp.transpose(w_in[0])
    perm = np.array([dr * SSM_HEADS + g * HEADS_PER_GROUP + r
                     for g in range(SSM_GROUPS) for dr in range(2) for r in range(HEADS_PER_GROUP)])
    w_dt_t = w_in_t[o_dt:o_dt + n_dt][perm].astype(BF16)
    dt_bias_col = dt_bias[0].reshape(n_dt)[perm][:, None]
    aneg_col = (-jnp.exp(a_log[0].astype(F32))).reshape(n_dt)[perm][:, None]

    x2d = x.reshape(n_lat, d)
    h_all, dt_c = _prenorm1(x2d, ctx.reshape(n_ctx_rows, d), norm1, mod3, batch, 0, 1, w_dt_t, dt_bias_col)

    tn = 1024
    tm_lat = 1024 if seq % 1024 == 0 else 512
    tm_all = n_all // 8 if (n_all // 8) % 16 == 0 and n_all % 8 == 0 else 512

    def in_proj(sections, rows, tm_rows, name):
        starts = np.cumsum([0] + [w // tn for _, w in sections])

        def row_off(j):
            off = sections[0][0] + j * tn
            for (first, _), s in zip(sections[1:], starts[1:]):
                off = jnp.where(j >= s, first + (j - s) * tn, off)
            return off

        return _matmul([(h_all, 0)], [(w_in_t, row_off, 0)], [], _ep_plain, int(starts[-1]) * tn, BF16,
                       m_rows=rows, tm=tm_rows, tn=tn, name=name, w_t=True)

    p_lat = in_proj([(0, d_ssm), (o_q, d_attn), (o_g, 2 * d)], n_lat, tm_lat, "proj_latent")
    p_all = in_proj([(o_xbc, d_conv), (o_k, 2 * d_kv)], n_all, tm_all, "proj_all")
    c_q = d_ssm
    c_g = d_ssm + d_attn
    c_k = d_conv
    c_v = d_conv + d_kv

    dskip_row = jnp.repeat(d_skip[0].astype(F32), SSM_HEAD_DIM)[None, :]
    y_norm = _ssd(p_all, conv_w[0], conv_b[0][None, :], dt_c, aneg_col, dskip_row, p_lat,
                  ssm_norm, batch, seq, n_ctx)

    cos_t, sin_t = _rope_tables(seq)
    attn = _attention(p_lat, c_q, p_all, c_k, c_v, q_norm, k_norm, cos_t, sin_t, batch, seq, n_ctx)

    tn_m = 512
    merged = _matmul([(y_norm, 0), (attn, 0)], [(w_ssm_br[0], 0, 0), (w_attn_br[0], 0, 1)],
                     [(p_lat, (tm_lat, tn_m), lambda j, m: (m, c_g // tn_m + j)),
                      (p_lat, (tm_lat, tn_m), lambda j, m: (m, (c_g + d) // tn_m + j))],
                     _ep_merge, d, BF16, m_rows=n_lat, tm=tm_lat, tn=tn_m, name="branch_merge")
    x_mid, h2 = _out_proj(merged, w_o[0], x2d, norm2, mod3, batch, 2, 3, 4)

    tn_f = 512
    act = _matmul([(h2, 0)], [(w_ffn_in[0], 0, 0), (w_ffn_in[0], d_ff // tn_f, 0)], [], _ep_swiglu,
                  d_ff, BF16, m_rows=n_lat, tm=tm_lat, tn=tn_f, name="ffn_in")
    tm_o = 512
    tiles_per_batch_o = seq // tm_o
    out = _matmul([(act, 0)], [(w_ffn_out[0], 0, 0)],
                  [(x_mid, (tm_o, tn_f), lambda j, m: (m, j)),
                   (mod3, (1, 1, tn_f), lambda j, m: (m // tiles_per_batch_o, 0, 5 * (d // tn_f) + j))],
                  _ep_gated_residual, d, F32, m_rows=n_lat, tm=tm_o, tn=tn_f, name="ffn_out")
    return out.reshape(batch, seq, d)
```

```python
import functools
import math

import jax
import jax.numpy as jnp
import numpy as np
from jax import lax
from jax.experimental import pallas as pl
from jax.experimental.pallas import tpu as pltpu

F32 = jnp.float32
BF16 = jnp.bfloat16

GRID_W = 64
SSM_HEADS = 32
SSM_HEAD_DIM = 64
SSM_GROUPS = 8
D_STATE = 128
CONV_W = 5
CHUNK = 128
ATTN_HEADS = 16
KV_HEADS = 4
HEAD_DIM = 128
ROPE_THETA = 10000.0
N_MOD = 6
EPS = 1e-6

HEADS_PER_GROUP = SSM_HEADS // SSM_GROUPS
GROUP_CH = HEADS_PER_GROUP * SSM_HEAD_DIM
Q_REP = ATTN_HEADS // KV_HEADS

VMEM_LIMIT_BYTES = 56 * 1024 * 1024
LOG2_E = 1.4426950408889634
SSD_UNROLL = 16


def _cparams(n_grid):
    return pltpu.CompilerParams(dimension_semantics=("arbitrary",) * n_grid,
                                vmem_limit_bytes=VMEM_LIMIT_BYTES)


def _silu(v):
    return v * jax.nn.sigmoid(v)


def _mod_kernel(c_ref, w_ref, b_ref, o_ref):
    s = _silu(c_ref[...]).astype(BF16)
    o_ref[...] = jnp.dot(s, w_ref[...].astype(BF16), preferred_element_type=F32) + b_ref[...]


def _modulation(cvec, w_mod, b_mod):
    rows, d = cvec.shape
    n = w_mod.shape[1]
    tn = 1024
    return pl.pallas_call(
        _mod_kernel,
        grid=(n // tn,),
        in_specs=[pl.BlockSpec((rows, d), lambda j: (0, 0)),
                  pl.BlockSpec((d, tn), lambda j: (0, j)),
                  pl.BlockSpec((1, tn), lambda j: (0, j))],
        out_specs=pl.BlockSpec((rows, tn), lambda j: (0, j)),
        out_shape=jax.ShapeDtypeStruct((rows, n), F32),
        compiler_params=_cparams(1),
        name="modulation",
    )(cvec, w_mod, b_mod)


PRENORM_ROWS = 512


NORM_ROWS = 16


def _norm_mod(t, g, sh, sc):
    ms = jnp.mean(t * t, axis=-1, keepdims=True)
    return ((t * lax.rsqrt(ms + EPS)) * g) * (1.0 + sc) + sh


def _norm_mod_rows(src_ref, dst_ref, g_ref, sh_ref, sc_ref):
    g = g_ref[...]
    sh = sh_ref[0]
    sc = sc_ref[0]

    def body(k, carry):
        rows = pl.ds(pl.multiple_of(k * NORM_ROWS, NORM_ROWS), NORM_ROWS)
        dst_ref[rows, :] = _norm_mod(src_ref[rows, :], g, sh, sc).astype(dst_ref.dtype)
        return carry

    lax.fori_loop(0, src_ref.shape[0] // NORM_ROWS, body, 0, unroll=8)


def _prenorm1_kernel(x_ref, ctx_ref, g_ref, sh_ref, sc_ref, wdt_ref, bdt_ref, o_ref, dt_ref, *, n_lat_tiles):
    i = pl.program_id(0)

    @pl.when(i < n_lat_tiles)
    def _():
        _norm_mod_rows(x_ref, o_ref, g_ref, sh_ref, sc_ref)

    @pl.when(i >= n_lat_tiles)
    def _():
        _norm_mod_rows(ctx_ref, o_ref, g_ref, sh_ref, sc_ref)

    raw = lax.dot_general(wdt_ref[...], o_ref[...], (((1,), (1,)), ((), ())),
                          preferred_element_type=F32) + bdt_ref[...]
    dt = jnp.maximum(raw, 0.0) + jnp.log1p(jnp.exp(-jnp.abs(raw)))
    for c in range(dt_ref.shape[0]):
        dt_ref[c] = dt[:, c * CHUNK:(c + 1) * CHUNK]


def _prenorm1(x2d, ctx2d, norm_g, mod3, batch, shift_blk, scale_blk, w_dt_t, dt_bias_col):
    n_lat, d = x2d.shape
    n_ctx = ctx2d.shape[0]
    n_dt = w_dt_t.shape[0]
    tr = math.gcd(PRENORM_ROWS, math.gcd(n_lat // batch, n_ctx))
    lat_tiles, ctx_tiles = n_lat // tr, n_ctx // tr
    tiles_per_batch = lat_tiles // batch

    def mod_row(i):
        return jnp.where(i < lat_tiles, i // tiles_per_batch, batch)

    return pl.pallas_call(
        functools.partial(_prenorm1_kernel, n_lat_tiles=lat_tiles),
        grid=(lat_tiles + ctx_tiles,),
        in_specs=[pl.BlockSpec((tr, d), lambda i: (jnp.minimum(i, lat_tiles - 1), 0)),
                  pl.BlockSpec((tr, d), lambda i: (jnp.maximum(i - lat_tiles, 0), 0)),
                  pl.BlockSpec((1, d), lambda i: (0, 0)),
                  pl.BlockSpec((1, 1, d), lambda i: (mod_row(i), 0, shift_blk)),
                  pl.BlockSpec((1, 1, d), lambda i: (mod_row(i), 0, scale_blk)),
                  pl.BlockSpec((n_dt, d), lambda i: (0, 0)),
                  pl.BlockSpec((n_dt, 1), lambda i: (0, 0))],
        out_specs=[pl.BlockSpec((tr, d), lambda i: (i, 0)),
                   pl.BlockSpec((tr // CHUNK, n_dt, CHUNK), lambda i: (i, 0, 0))],
        out_shape=[jax.ShapeDtypeStruct((n_lat + n_ctx, d), BF16),
                   jax.ShapeDtypeStruct(((n_lat + n_ctx) // CHUNK, n_dt, CHUNK), F32)],
        compiler_params=_cparams(1),
        name="prenorm1",
    )(x2d, ctx2d, norm_g, mod3, mod3, w_dt_t, dt_bias_col)


OUT_PROJ_ROWS = 512
OUT_PROJ_SECTIONS = 2


def _out_proj_kernel(a_ref, w_ref, x_ref, gt_ref, g_ref, sh_ref, sc_ref, xmid_ref, h_ref, w_scr):
    @pl.when(pl.program_id(0) == 0)
    def _():
        w_scr[...] = w_ref[...].astype(BF16)

    half = a_ref.shape[0] // OUT_PROJ_SECTIONS
    g, sh, sc, gt = g_ref[...], sh_ref[0], sc_ref[0], gt_ref[0]
    for h0 in range(0, a_ref.shape[0], half):
        xmid_ref[h0:h0 + half, :] = x_ref[h0:h0 + half, :] + gt * jnp.dot(
            a_ref[h0:h0 + half, :], w_scr[...], preferred_element_type=F32)
        for r in range(h0, h0 + half, NORM_ROWS):
            h_ref[r:r + NORM_ROWS, :] = _norm_mod(xmid_ref[r:r + NORM_ROWS, :], g, sh, sc).astype(h_ref.dtype)


def _out_proj(a, w, x2d, norm_g, mod3, batch, gate_blk, shift_blk, scale_blk):
    n_lat, d = x2d.shape
    tr = math.gcd(OUT_PROJ_ROWS, n_lat // batch)
    tiles_per_batch = n_lat // tr // batch

    def mod_spec(blk):
        return pl.BlockSpec((1, 1, d), lambda i: (i // tiles_per_batch, 0, blk))

    return pl.pallas_call(
        _out_proj_kernel,
        grid=(n_lat // tr,),
        in_specs=[pl.BlockSpec((tr, a.shape[1]), lambda i: (i, 0)),
                  pl.BlockSpec(w.shape, lambda i: (0, 0), pipeline_mode=pl.Buffered(1)),
                  pl.BlockSpec((tr, d), lambda i: (i, 0)),
                  mod_spec(gate_blk),
                  pl.BlockSpec((1, d), lambda i: (0, 0)),
                  mod_spec(shift_blk),
                  mod_spec(scale_blk)],
        out_specs=[pl.BlockSpec((tr, d), lambda i: (i, 0)),
                   pl.BlockSpec((tr, d), lambda i: (i, 0))],
        out_shape=[jax.ShapeDtypeStruct((n_lat, d), F32),
                   jax.ShapeDtypeStruct((n_lat, d), BF16)],
        scratch_shapes=[pltpu.VMEM(w.shape, BF16)],
        compiler_params=_cparams(1),
        name="out_proj",
    )(a, w, x2d, mod3, norm_g, mod3, mod3)


def _mm_kernel(*refs, n_a, which_a, n_extra, epilogue, w_t):
    n_w = len(which_a)
    a_refs = refs[:n_a]
    w_refs = refs[n_a:n_a + n_w]
    e_refs = refs[n_a + n_w:n_a + n_w + n_extra]
    o_ref = refs[n_a + n_w + n_extra]
    w_scr = refs[n_a + n_w + n_extra + 1:]

    @pl.when(pl.program_id(1) == 0)
    def _():
        for k in range(n_w):
            w_scr[k][...] = w_refs[k][...].astype(BF16)

    dims = (((1,), (1,)), ((), ())) if w_t else (((1,), (0,)), ((), ()))
    accs = [lax.dot_general(a_refs[which_a[k]][...], w_scr[k][...], dims, preferred_element_type=F32)
            for k in range(n_w)]
    o_ref[...] = epilogue(accs, e_refs).astype(o_ref.dtype)


def _matmul(a_list, w_list, extras, epilogue, n_out, out_dtype, *, m_rows, tm, tn, name, w_t=False):
    grid = (n_out // tn, m_rows // tm)
    in_specs, args, scratch = [], [], []
    for a, off in a_list:
        in_specs.append(pl.BlockSpec((tm, a.shape[1]), lambda j, m, off=off: (m + off, 0)))
        args.append(a)
    for w, off, _ in w_list:
        if w_t:
            in_specs.append(pl.BlockSpec((pl.Element(tn), pl.Element(w.shape[1])),
                                         lambda j, m, off=off: (pl.multiple_of(off(j), 8), 0)))
            scratch.append(pltpu.VMEM((tn, w.shape[1]), BF16))
        else:
            in_specs.append(pl.BlockSpec((w.shape[0], tn), lambda j, m, off=off: (0, j + off)))
            scratch.append(pltpu.VMEM((w.shape[0], tn), BF16))
        args.append(w)
    for e, blk, imap in extras:
        in_specs.append(pl.BlockSpec(blk, imap))
        args.append(e)
    kern = functools.partial(_mm_kernel, n_a=len(a_list), which_a=tuple(w[2] for w in w_list),
                             n_extra=len(extras), epilogue=epilogue, w_t=w_t)
    return pl.pallas_call(
        kern,
        grid=grid,
        in_specs=in_specs,
        out_specs=pl.BlockSpec((tm, tn), lambda j, m: (m, j)),
        out_shape=jax.ShapeDtypeStruct((m_rows, n_out), out_dtype),
        scratch_shapes=scratch,
        compiler_params=_cparams(2),
        name=name,
    )(*args)


def _ep_plain(accs, e_refs):
    return accs[0]


def _ep_swiglu(accs, e_refs):
    return _silu(accs[0]) * accs[1]


def _ep_merge(accs, e_refs):
    g_ssm = e_refs[0][...].astype(F32)
    g_attn = e_refs[1][...].astype(F32)
    return jax.nn.sigmoid(g_ssm) * accs[0] + jax.nn.sigmoid(g_attn) * accs[1]


def _ep_gated_residual(accs, e_refs):
    return e_refs[0][...] + e_refs[1][0] * accs[0]


def _norm_rope(t, g, cs, sn, scale):
    tn = (t * lax.rsqrt(jnp.mean(t * t, axis=-1, keepdims=True) + EPS)) * g
    lane = lax.broadcasted_iota(jnp.int32, tn.shape, 1)
    first_half = (lane % (HEAD_DIM // 2)) < (HEAD_DIM // 4)
    rot = jnp.where(first_half,
                    pltpu.roll(tn, HEAD_DIM - HEAD_DIM // 4, 1),
                    pltpu.roll(tn, HEAD_DIM // 4, 1))
    return (tn * cs + rot * sn) * scale


def _attn_kernel(q_ref, kl_ref, kc_ref, vl_ref, vc_ref, qg_ref, kg_ref, cos_ref, sin_ref, o_ref,
                 k_scr, v_scr, *bufs, n_ctx, tq, tk, q_scale):
    seq = q_ref.shape[0]
    n_tiles = seq // tq
    kg = kg_ref[...]
    qg = qg_ref[...]

    kc = kc_ref[...].astype(F32)
    k_scr[0:n_ctx, :] = ((kc * lax.rsqrt(jnp.mean(kc * kc, axis=-1, keepdims=True) + EPS)) * kg).astype(BF16)

    def k_body(i, carry):
        r0 = pl.multiple_of(i * tk, tk)
        kt = _norm_rope(kl_ref[pl.ds(r0, tk), :].astype(F32), kg,
                        cos_ref[pl.ds(r0, tk), :], sin_ref[pl.ds(r0, tk), :], 1.0)
        k_scr[pl.ds(pl.multiple_of(n_ctx + r0, tk), tk), :] = kt.astype(BF16)
        return carry

    lax.fori_loop(0, seq // tk, k_body, 0, unroll=2)
    v_scr[0:n_ctx, 0:HEAD_DIM] = vc_ref[...]
    v_scr[n_ctx:, 0:HEAD_DIM] = vl_ref[...]
    v_scr[:, HEAD_DIM:] = jnp.ones((v_scr.shape[0], HEAD_DIM), BF16)

    s_bufs = bufs[:Q_REP]
    p_bufs = bufs[Q_REP:]

    def scores(t, r):
        r0 = pl.multiple_of(t * tq, tq)
        q = _norm_rope(q_ref[pl.ds(r0, tq), r * HEAD_DIM:(r + 1) * HEAD_DIM].astype(F32), qg,
                       cos_ref[pl.ds(r0, tq), :], sin_ref[pl.ds(r0, tq), :], q_scale).astype(BF16)
        s_bufs[r][...] = lax.dot_general(q, k_scr[...], (((1,), (1,)), ((), ())),
                                             preferred_element_type=F32)

    def softmax(r):
        sc = s_bufs[r][...]
        mx = jnp.max(sc, axis=-1, keepdims=True)
        p_bufs[r][...] = jnp.exp2((sc - mx).astype(BF16))

    def values(t, r):
        r0 = pl.multiple_of(t * tq, tq)
        o = jnp.dot(p_bufs[r][...], v_scr[...], preferred_element_type=F32)
        den = o[:, HEAD_DIM:HEAD_DIM + 1]
        o_ref[pl.ds(r0, tq), r * HEAD_DIM:(r + 1) * HEAD_DIM] = (o[:, :HEAD_DIM] / den).astype(o_ref.dtype)

    def tile(t, first):
        for r in range(Q_REP):
            if r >= 1:
                softmax(r - 1)
            elif not first:
                softmax(Q_REP - 1)
            if r >= 2:
                values(t, r - 2)
            elif not first:
                values(t - 1, Q_REP - 2 + r)
            scores(t, r)

    tile(0, True)
    n_peel = 1 + (n_tiles - 1) % 2
    for t in range(1, n_peel):
        tile(t, False)

    def body(t, carry):
        tile(t, False)
        return carry

    lax.fori_loop(n_peel, n_tiles, body, 0, unroll=2)
    softmax(Q_REP - 1)
    values(n_tiles - 1, Q_REP - 2)
    values(n_tiles - 1, Q_REP - 1)


def _attention(p_lat, c_q, p_all, c_k, c_v, q_norm, k_norm, cos_t, sin_t, batch, seq, n_ctx):
    tq = 128
    tk = 256
    assert Q_REP >= 2
    ctx_blk0 = batch * seq // n_ctx
    gw = Q_REP * HEAD_DIM
    n_keys = n_ctx + seq
    qb, kb, vb = c_q // gw, c_k // HEAD_DIM, c_v // HEAD_DIM
    const = dict(pipeline_mode=pl.Buffered(1))
    return pl.pallas_call(
        functools.partial(_attn_kernel, n_ctx=n_ctx, tq=tq, tk=tk, q_scale=HEAD_DIM ** -0.5 * LOG2_E),
        grid=(batch, KV_HEADS),
        in_specs=[pl.BlockSpec((seq, gw), lambda b, g: (b, qb + g)),
                  pl.BlockSpec((seq, HEAD_DIM), lambda b, g: (b, kb + g)),
                  pl.BlockSpec((n_ctx, HEAD_DIM), lambda b, g: (ctx_blk0 + b, kb + g)),
                  pl.BlockSpec((seq, HEAD_DIM), lambda b, g: (b, vb + g)),
                  pl.BlockSpec((n_ctx, HEAD_DIM), lambda b, g: (ctx_blk0 + b, vb + g)),
                  pl.BlockSpec((1, HEAD_DIM), lambda b, g: (0, 0)),
                  pl.BlockSpec((1, HEAD_DIM), lambda b, g: (0, 0)),
                  pl.BlockSpec((seq, HEAD_DIM), lambda b, g: (0, 0), **const),
                  pl.BlockSpec((seq, HEAD_DIM), lambda b, g: (0, 0), **const)],
        out_specs=pl.BlockSpec((seq, gw), lambda b, g: (b, g)),
        out_shape=jax.ShapeDtypeStruct((batch * seq, ATTN_HEADS * HEAD_DIM), BF16),
        scratch_shapes=[pltpu.VMEM((n_keys, HEAD_DIM), BF16),
                        pltpu.VMEM((n_keys, 2 * HEAD_DIM), BF16),
                        *[pltpu.VMEM((tq, n_keys), F32) for _ in range(Q_REP)],
                        *[pltpu.VMEM((tq, n_keys), BF16) for _ in range(Q_REP)]],
        compiler_params=_cparams(2),
        name="attention",
    )(p_lat, p_all, p_all, p_all, p_all, q_norm, k_norm, cos_t, sin_t)


def _split3(v):
    hi = v.astype(BF16).astype(F32)
    r1 = v - hi
    mid = r1.astype(BF16).astype(F32)
    lo = (r1 - mid).astype(BF16).astype(F32)
    return hi, mid, lo


CONV_HALO = 16


def _conv_shift_matrix():
    pad = (CONV_W - 1) // 2
    taps = [j for j in range(CONV_W) if j != pad]
    rows = lax.broadcasted_iota(jnp.int32, (len(taps) * CHUNK, CHUNK + 2 * CONV_HALO), 0)
    cols = lax.broadcasted_iota(jnp.int32, (len(taps) * CHUNK, CHUNK + 2 * CONV_HALO), 1)
    tap = jnp.where(rows // CHUNK < pad, rows // CHUNK, rows // CHUNK + 1)
    return (cols == CONV_HALO - pad + tap + rows % CHUNK).astype(BF16), taps


def _conv_block_fn(streams, n_rows, shift, taps):
    n_blk = n_rows // CHUNK
    pad = (CONV_W - 1) // 2

    def lanes(parts):
        return parts[0] if len(parts) == 1 else jnp.concatenate(parts, axis=1)

    def block(i):
        r0 = pl.multiple_of(i * CHUNK, CHUNK)
        lo = pl.multiple_of(jnp.maximum(r0 - CONV_HALO, 0), CONV_HALO)
        hi = pl.multiple_of(jnp.minimum(r0 + CHUNK, n_rows - CONV_HALO), CONV_HALO)
        out = []
        for src_refs, w_refs, b_refs, store in streams:
            cur = lanes([s[pl.ds(r0, CHUNK), :] for s in src_refs])
            prev = lanes([s[pl.ds(lo, CONV_HALO), :] for s in src_refs])
            nxt = lanes([s[pl.ds(hi, CONV_HALO), :] for s in src_refs])
            prev = jnp.where(i > 0, prev, jnp.zeros_like(prev))
            nxt = jnp.where(i < n_blk - 1, nxt, jnp.zeros_like(nxt))
            window = jnp.concatenate([prev, cur, nxt], axis=0)
            shifted = jnp.dot(shift, window, preferred_element_type=F32)
            w = lanes([r[...] for r in w_refs])
            acc = lanes([r[...] for r in b_refs]) + w[pad:pad + 1, :] * cur.astype(F32)
            for k, j in enumerate(taps):
                acc = acc + w[j:j + 1, :] * shifted[k * CHUNK:(k + 1) * CHUNK, :]
            out.append(store(r0, _silu(acc)))
        return out

    return block


def _ssd_kernel(x_lat, b_lat, c_lat, x_ctx, b_ctx, c_ctx,
                cwx, cwb, cwc, cbx, cbb, cbc,
                dt_lat, dt_ctx, aneg_ref, dskip_ref, z_ref, ng_ref,
                o_ref,
                xc, bt, cc, y_scr, dt_scr, u_scr, xt_scr, st_f, st_b, *, seq, n_ctx):
    nh = HEADS_PER_GROUP
    n_ctx_chunks = n_ctx // CHUNK
    n_lat_chunks = seq // CHUNK
    n_chunks = n_ctx_chunks + n_lat_chunks

    def store_x(off):
        def f(r0, v):
            xc[pl.ds(pl.multiple_of(off + r0, CHUNK), CHUNK), :] = v
            return v
        return f

    def store_bc(off):
        def f(r0, v):
            r = pl.multiple_of(off + r0, CHUNK)
            b_t = v[:, :D_STATE].T.astype(BF16)
            c_m = v[:, D_STATE:].astype(BF16)
            bt[:, pl.ds(r, CHUNK)] = b_t
            cc[pl.ds(r, CHUNK), :] = c_m
            return b_t, c_m
        return f

    shift, taps = _conv_shift_matrix()
    conv_ctx = _conv_block_fn([([x_ctx], [cwx], [cbx], store_x(0)),
                               ([b_ctx, c_ctx], [cwb, cwc], [cbb, cbc], store_bc(0))], n_ctx, shift, taps)
    conv_lat = _conv_block_fn([([x_lat], [cwx], [cbx], store_x(n_ctx)),
                               ([b_lat, c_lat], [cwb, cwc], [cbb, cbc], store_bc(n_ctx))], seq, shift, taps)
    for c in range(n_ctx_chunks):
        conv_ctx(c)

    row = lax.broadcasted_iota(jnp.int32, (CHUNK, CHUNK), 0)
    col = lax.broadcasted_iota(jnp.int32, (CHUNK, CHUNK), 1)
    triu = (row <= col).astype(BF16)
    eye = (row == col).astype(BF16)
    lower = row >= col
    below = row > col
    above = row < col
    dskip = dskip_ref[...]
    lane_head = lax.broadcasted_iota(jnp.int32, (CHUNK, GROUP_CH), 1) // SSM_HEAD_DIM

    def spread(n_cols, q_of_col, hd_of_col):
        k = lax.broadcasted_iota(jnp.int32, (CHUNK, n_cols), 0)
        c = lax.broadcasted_iota(jnp.int32, (CHUNK, n_cols), 1)
        return ((k < 72) & (k // 24 == q_of_col(c)) & (k % 8 == hd_of_col(c))).astype(BF16)

    n_cb = 2 * nh * CHUNK
    rmat_f = spread(n_cb + 2 * GROUP_CH,
                    lambda c: jnp.where(c < n_cb, 0, jnp.where(c < n_cb + GROUP_CH, 1, 2)),
                    lambda c: jnp.where(c < n_cb, c // CHUNK, ((c - n_cb) % GROUP_CH) // SSM_HEAD_DIM))
    rmat_b = spread(2 * GROUP_CH,
                    lambda c: jnp.where(c < GROUP_CH, 1, 2),
                    lambda c: nh + (c % GROUP_CH) // SSM_HEAD_DIM)

    dt_scr[0:n_ctx_chunks] = dt_ctx[...]
    dt_scr[n_ctx_chunks:n_chunks] = dt_lat[...]
    dt3 = dt_scr[...]
    n_rows = n_chunks * 8
    dt2 = dt3.reshape(n_rows, CHUNK)
    a2 = (dt3 * aneg_ref[...][None]).reshape(n_rows, CHUNK)
    cum = sum(jnp.dot(part.astype(BF16), triu, preferred_element_type=F32) for part in _split3(a2))
    total = jnp.broadcast_to(cum[:, CHUNK - 1:CHUNK], cum.shape)
    cumx = cum - a2
    is_fwd = (lax.broadcasted_iota(jnp.int32, (n_rows, CHUNK), 0) % 8) < nh
    u2 = jnp.where(is_fwd, cum, cumx) * LOG2_E
    w2 = dt2 * jnp.exp(jnp.where(is_fwd, total - cum, cumx))
    e2 = jnp.exp(jnp.where(is_fwd, cum, total - cumx))
    u_scr[...] = u2.reshape(n_chunks, 8, CHUNK)
    pieces = [p.reshape(n_chunks, 8, CHUNK) for v in (u2, w2, e2) for p in _split3(v)]
    pieces.append(jnp.zeros((n_chunks, CHUNK - 8 * len(pieces), CHUNK), F32))
    packed = jnp.concatenate(pieces, axis=1).reshape(n_chunks * CHUNK, CHUNK).astype(BF16)
    xt_scr[...] = lax.dot_general(eye, packed, (((1,), (1,)), ((), ())),
                                  preferred_element_type=F32).astype(BF16)

    def finish(y_off_row, y):
        y = y + y_scr[pl.ds(y_off_row, CHUNK), :]
        gated = y * _silu(z_ref[pl.ds(y_off_row, CHUNK), :].astype(F32))
        ms = jnp.mean(gated * gated, axis=-1, keepdims=True)
        o_ref[pl.ds(y_off_row, CHUNK), :] = ((gated * lax.rsqrt(ms + EPS)) * ng_ref[...]).astype(o_ref.dtype)

    def emit(y_off_row, y, last):
        if last:
            finish(y_off_row, y)
        else:
            y_scr[pl.ds(y_off_row, CHUNK), :] = y

    def fwd_chunk(ci, y_off_row, st_prev, conv_vals=None, last=False):
        r0 = pl.multiple_of(ci * CHUNK, CHUNK)
        xt = xt_scr[:, pl.ds(r0, CHUNK)]
        if conv_vals is None:
            xs = xc[pl.ds(r0, CHUNK), :]
            btc = bt[:, pl.ds(r0, CHUNK)]
        else:
            xs, (btc, cm) = conv_vals
        if y_off_row is None:
            big = jnp.dot(xt, rmat_f[:, n_cb:], preferred_element_type=F32)
            w_col, e_col = big[:, :GROUP_CH], big[:, GROUP_CH:]
        else:
            big = jnp.dot(xt, rmat_f, preferred_element_type=F32)
            cb, w_col, e_col = big[:, :n_cb], big[:, n_cb:n_cb + GROUP_CH], big[:, n_cb + GROUP_CH:]
            u8 = u_scr[ci]
            dt8 = dt_scr[ci]
            if conv_vals is None:
                cm = cc[pl.ds(r0, CHUNK), :]
            g = jnp.dot(cm, btc, preferred_element_type=F32)
            xs_b = xs.astype(BF16)
            w_parts, x_parts = [], []
            for r in range(nh):
                ef = cb[:, r * CHUNK:(r + 1) * CHUNK] - u8[r:r + 1, :]
                eb = u8[nh + r:nh + r + 1, :] - cb[:, (nh + r) * CHUNK:(nh + r + 1) * CHUNK]
                dt_f = dt8[r:r + 1, :]
                dt_b = dt8[nh + r:nh + r + 1, :]
                dt_sel = jnp.where(below, dt_f, jnp.where(above, dt_b, dt_f + dt_b))
                w_parts.append((g * (jnp.exp2(jnp.where(lower, ef, eb)) * dt_sel)).astype(BF16))
                x_parts.append(jnp.where(lane_head == r, xs_b, jnp.zeros_like(xs_b)))
            y = jnp.dot(jnp.concatenate(w_parts, axis=1), jnp.concatenate(x_parts, axis=0),
                        preferred_element_type=F32)
            y = y + e_col * jnp.dot(cm, st_prev.astype(BF16), preferred_element_type=F32)
            emit(y_off_row, y + dskip * xs, last)
        xw = (xs * w_col).astype(BF16)
        return st_prev * e_col[CHUNK - 1:CHUNK, :] + jnp.dot(btc, xw, preferred_element_type=F32)

    def bwd_chunk(ci, y_off_row, st_prev, conv_vals=None, last=False):
        r0 = pl.multiple_of(ci * CHUNK, CHUNK)
        xt = xt_scr[:, pl.ds(r0, CHUNK)]
        if conv_vals is None:
            xs = xc[pl.ds(r0, CHUNK), :]
            btc = bt[:, pl.ds(r0, CHUNK)]
        else:
            xs, (btc, cm) = conv_vals
        big = jnp.dot(xt, rmat_b, preferred_element_type=F32)
        w_col, e_col = big[:, :GROUP_CH], big[:, GROUP_CH:]
        if y_off_row is not None:
            if conv_vals is None:
                cm = cc[pl.ds(r0, CHUNK), :]
            emit(y_off_row, e_col * jnp.dot(cm, st_prev.astype(BF16), preferred_element_type=F32), last)
        xw = (xs * w_col).astype(BF16)
        return st_prev * e_col[0:1, :] + jnp.dot(btc, xw, preferred_element_type=F32)

    st = jnp.zeros((D_STATE, GROUP_CH), F32)
    for c in range(n_ctx_chunks):
        st = fwd_chunk(c, None, st)
    st_f[...] = st
    st = jnp.zeros((D_STATE, GROUP_CH), F32)
    for c in reversed(range(n_ctx_chunks)):
        st = bwd_chunk(c, None, st)
    st_b[...] = st

    assert n_lat_chunks % 2 == 0
    half = n_lat_chunks // 2
    unroll = math.gcd(SSD_UNROLL, half)

    def row_of(c):
        return pl.multiple_of(c * CHUNK, CHUNK)

    def first_half(i, carry):
        sf, sb = st_f[...], st_b[...]
        vf = conv_lat(i * unroll)
        vb = conv_lat(n_lat_chunks - 1 - i * unroll)
        for u in range(unroll):
            kf = i * unroll + u
            kb = n_lat_chunks - 1 - kf
            nf = conv_lat(kf + 1) if u + 1 < unroll else None
            nb = conv_lat(kb - 1) if u + 1 < unroll else None
            sf = fwd_chunk(n_ctx_chunks + kf, row_of(kf), sf, vf)
            sb = bwd_chunk(n_ctx_chunks + kb, row_of(kb), sb, vb)
            vf, vb = nf, nb
        st_f[...] = sf
        st_b[...] = sb
        return carry

    lax.fori_loop(0, half // unroll, first_half, 0)

    def second_half(i, carry):
        sf, sb = st_f[...], st_b[...]
        for u in range(unroll):
            kf = half + i * unroll + u
            kb = n_lat_chunks - 1 - kf
            sf = fwd_chunk(n_ctx_chunks + kf, row_of(kf), sf, last=True)
            sb = bwd_chunk(n_ctx_chunks + kb, row_of(kb), sb, last=True)
        st_f[...] = sf
        st_b[...] = sb
        return carry

    lax.fori_loop(0, half // unroll, second_half, 0)


def _ssd(xbc_all, conv_w, conv_b, dt_c, aneg_col, dskip_row, z, norm_g, batch, seq, n_ctx):
    d_ssm = SSM_HEADS * SSM_HEAD_DIM
    b0 = d_ssm // D_STATE
    c0 = b0 + SSM_GROUPS
    ctx_blk0 = batch * seq // n_ctx
    n_ctx_chunks, n_lat_chunks = n_ctx // CHUNK, seq // CHUNK
    n_tok = n_ctx + seq
    in_specs = [
        pl.BlockSpec((seq, GROUP_CH), lambda b, g: (b, g)),
        pl.BlockSpec((seq, D_STATE), lambda b, g: (b, b0 + g)),
        pl.BlockSpec((seq, D_STATE), lambda b, g: (b, c0 + g)),
        pl.BlockSpec((n_ctx, GROUP_CH), lambda b, g: (ctx_blk0 + b, g)),
        pl.BlockSpec((n_ctx, D_STATE), lambda b, g: (ctx_blk0 + b, b0 + g)),
        pl.BlockSpec((n_ctx, D_STATE), lambda b, g: (ctx_blk0 + b, c0 + g)),
        pl.BlockSpec((CONV_W, GROUP_CH), lambda b, g: (0, g)),
        pl.BlockSpec((CONV_W, D_STATE), lambda b, g: (0, b0 + g)),
        pl.BlockSpec((CONV_W, D_STATE), lambda b, g: (0, c0 + g)),
        pl.BlockSpec((1, GROUP_CH), lambda b, g: (0, g)),
        pl.BlockSpec((1, D_STATE), lambda b, g: (0, b0 + g)),
        pl.BlockSpec((1, D_STATE), lambda b, g: (0, c0 + g)),
        pl.BlockSpec((n_lat_chunks, 8, CHUNK), lambda b, g: (b, g, 0)),
        pl.BlockSpec((n_ctx_chunks, 8, CHUNK), lambda b, g: (ctx_blk0 + b, g, 0)),
        pl.BlockSpec((8, 1), lambda b, g: (g, 0)),
        pl.BlockSpec((1, GROUP_CH), lambda b, g: (0, g)),
        pl.BlockSpec((seq, GROUP_CH), lambda b, g: (b, g)),
        pl.BlockSpec((1, GROUP_CH), lambda b, g: (0, g)),
    ]
    return pl.pallas_call(
        functools.partial(_ssd_kernel, seq=seq, n_ctx=n_ctx),
        grid=(batch, SSM_GROUPS),
        in_specs=in_specs,
        out_specs=pl.BlockSpec((seq, GROUP_CH), lambda b, g: (b, g)),
        out_shape=jax.ShapeDtypeStruct((batch * seq, d_ssm), BF16),
        scratch_shapes=[pltpu.VMEM((n_tok, GROUP_CH), F32),
                        pltpu.VMEM((D_STATE, n_tok), BF16),
                        pltpu.VMEM((n_tok, D_STATE), BF16),
                        pltpu.VMEM((seq, GROUP_CH), F32),
                        pltpu.VMEM((n_tok // CHUNK, 8, CHUNK), F32),
                        pltpu.VMEM((n_tok // CHUNK, 8, CHUNK), F32),
                        pltpu.VMEM((CHUNK, n_tok), BF16),
                        pltpu.VMEM((D_STATE, GROUP_CH), F32),
                        pltpu.VMEM((D_STATE, GROUP_CH), F32)],
        compiler_params=_cparams(2),
        name="ssd",
    )(xbc_all, xbc_all, xbc_all, xbc_all, xbc_all, xbc_all,
      conv_w, conv_w, conv_w, conv_b, conv_b, conv_b,
      dt_c, dt_c, aneg_col, dskip_row, z, norm_g)


def _rope_tables(seq):
    n_freq = HEAD_DIM // 4
    pos = np.arange(seq)
    inv = np.power(ROPE_THETA, -np.arange(n_freq, dtype=np.float64) / n_freq)
    ang_r = (pos // GRID_W)[:, None] * inv
    ang_c = (pos % GRID_W)[:, None] * inv
    cos_t = np.concatenate([np.cos(ang_r), np.cos(ang_r), np.cos(ang_c), np.cos(ang_c)], axis=1)
    sin_t = np.concatenate([-np.sin(ang_r), np.sin(ang_r), -np.sin(ang_c), np.sin(ang_c)], axis=1)
    return jnp.asarray(cos_t, F32), jnp.asarray(sin_t, F32)


def kernel(x, c, ctx, c_ctx, w_mod, b_mod, norm1, w_in, conv_w, conv_b, dt_bias, a_log, d_skip,
           ssm_norm, q_norm, k_norm, w_ssm_br, w_attn_br, w_o, norm2, w_ffn_in, w_ffn_out):
    batch, seq, d = x.shape
    n_ctx = ctx.shape[1]
    n_lat = batch * seq
    n_ctx_rows = batch * n_ctx
    n_all = n_lat + n_ctx_rows
    d_ssm = SSM_HEADS * SSM_HEAD_DIM
    d_conv = d_ssm + 2 * SSM_GROUPS * D_STATE
    d_attn = ATTN_HEADS * HEAD_DIM
    d_kv = KV_HEADS * HEAD_DIM
    d_ff = w_ffn_out.shape[1]
    n_dt = 2 * SSM_HEADS
    o_xbc = d_ssm
    o_dt = o_xbc + d_conv
    o_q = o_dt + n_dt
    o_k = o_q + d_attn
    o_g = o_k + 2 * d_kv
    assert w_mod.shape[0] == 1, "single-layer block"

    cvec = jnp.concatenate([c, c_ctx[None, :], jnp.zeros((8 - batch - 1, d), F32)], axis=0)
    mod = _modulation(cvec, w_mod[0], b_mod[0][None, :])
    mod3 = mod[:batch + 1].reshape(batch + 1, 1, N_MOD * d)

    w_in_t = jnp.transpose(w_in[0])
    perm = np.array([dr * SSM_HEADS + g * HEADS_PER_GROUP + r
                     for g in range(SSM_GROUPS) for dr in range(2) for r in range(HEADS_PER_GROUP)])
    w_dt_t = w_in_t[o_dt:o_dt + n_dt][perm].astype(BF16)
    dt_bias_col = dt_bias[0].reshape(n_dt)[perm][:, None]
    aneg_col = (-jnp.exp(a_log[0].astype(F32))).reshape(n_dt)[perm][:, None]

    x2d = x.reshape(n_lat, d)
    h_all, dt_c = _prenorm1(x2d, ctx.reshape(n_ctx_rows, d), norm1, mod3, batch, 0, 1, w_dt_t, dt_bias_col)

    tn = 1024
    tm_lat = 1024 if seq % 1024 == 0 else 512
    tm_all = n_all // 8 if (n_all // 8) % 16 == 0 and n_all % 8 == 0 else 512

    def in_proj(sections, rows, tm_rows, name):
        starts = np.cumsum([0] + [w // tn for _, w in sections])

        def row_off(j):
            off = sections[0][0] + j * tn
            for (first, _), s in zip(sections[1:], starts[1:]):
                off = jnp.where(j >= s, first + (j - s) * tn, off)
            return off

        return _matmul([(h_all, 0)], [(w_in_t, row_off, 0)], [], _ep_plain, int(starts[-1]) * tn, BF16,
                       m_rows=rows, tm=tm_rows, tn=tn, name=name, w_t=True)

    p_lat = in_proj([(0, d_ssm), (o_q, d_attn), (o_g, 2 * d)], n_lat, tm_lat, "proj_latent")
    p_all = in_proj([(o_xbc, d_conv), (o_k, 2 * d_kv)], n_all, tm_all, "proj_all")
    c_q = d_ssm
    c_g = d_ssm + d_attn
    c_k = d_conv
    c_v = d_conv + d_kv

    dskip_row = jnp.repeat(d_skip[0].astype(F32), SSM_HEAD_DIM)[None, :]
    y_norm = _ssd(p_all, conv_w[0], conv_b[0][None, :], dt_c, aneg_col, dskip_row, p_lat,
                  ssm_norm, batch, seq, n_ctx)

    cos_t, sin_t = _rope_tables(seq)
    attn = _attention(p_lat, c_q, p_all, c_k, c_v, q_norm, k_norm, cos_t, sin_t, batch, seq, n_ctx)

    tn_m = 512
    merged = _matmul([(y_norm, 0), (attn, 0)], [(w_ssm_br[0], 0, 0), (w_attn_br[0], 0, 1)],
                     [(p_lat, (tm_lat, tn_m), lambda j, m: (m, c_g // tn_m + j)),
                      (p_lat, (tm_lat, tn_m), lambda j, m: (m, (c_g + d) // tn_m + j))],
                     _ep_merge, d, BF16, m_rows=n_lat, tm=tm_lat, tn=tn_m, name="branch_merge")
    x_mid, h2 = _out_proj(merged, w_o[0], x2d, norm2, mod3, batch, 2, 3, 4)

    tn_f = 512
    act = _matmul([(h2, 0)], [(w_ffn_in[0], 0, 0), (w_ffn_in[0], d_ff // tn_f, 0)], [], _ep_swiglu,
                  d_ff, BF16, m_rows=n_lat, tm=tm_lat, tn=tn_f, name="ffn_in")
    tm_o = 512
    tiles_per_batch_o = seq // tm_o
    out = _matmul([(act, 0)], [(w_ffn_out[0], 0, 0)],
                  [(x_mid, (tm_o, tn_f), lambda j, m: (m, j)),
                   (mod3, (1, 1, tn_f), lambda j, m: (m // tiles_per_batch_o, 0, 5 * (d // tn_f) + j))],
                  _ep_gated_residual, d, F32, m_rows=n_lat, tm=tm_o, tn=tn_f, name="ffn_out")
    return out.reshape(batch, seq, d)
```

```python
import functools
import math

import jax
import jax.numpy as jnp
import numpy as np
from jax import lax
from jax.experimental import pallas as pl
from jax.experimental.pallas import tpu as pltpu

F32 = jnp.float32
BF16 = jnp.bfloat16

GRID_W = 64
SSM_HEADS = 32
SSM_HEAD_DIM = 64
SSM_GROUPS = 8
D_STATE = 128
CONV_W = 5
CHUNK = 128
ATTN_HEADS = 16
KV_HEADS = 4
HEAD_DIM = 128
ROPE_THETA = 10000.0
N_MOD = 6
EPS = 1e-6

HEADS_PER_GROUP = SSM_HEADS // SSM_GROUPS
GROUP_CH = HEADS_PER_GROUP * SSM_HEAD_DIM
Q_REP = ATTN_HEADS // KV_HEADS

VMEM_LIMIT_BYTES = 56 * 1024 * 1024
LOG2_E = 1.4426950408889634
SSD_UNROLL = 16


def _cparams(n_grid):
    return pltpu.CompilerParams(dimension_semantics=("arbitrary",) * n_grid,
                                vmem_limit_bytes=VMEM_LIMIT_BYTES)


def _silu(v):
    return v * jax.nn.sigmoid(v)


def _mod_kernel(c_ref, w_ref, b_ref, o_ref):
    s = _silu(c_ref[...]).astype(BF16)
    o_ref[...] = jnp.dot(s, w_ref[...].astype(BF16), preferred_element_type=F32) + b_ref[...]


def _modulation(cvec, w_mod, b_mod):
    rows, d = cvec.shape
    n = w_mod.shape[1]
    tn = 1024
    return pl.pallas_call(
        _mod_kernel,
        grid=(n // tn,),
        in_specs=[pl.BlockSpec((rows, d), lambda j: (0, 0)),
                  pl.BlockSpec((d, tn), lambda j: (0, j)),
                  pl.BlockSpec((1, tn), lambda j: (0, j))],
        out_specs=pl.BlockSpec((rows, tn), lambda j: (0, j)),
        out_shape=jax.ShapeDtypeStruct((rows, n), F32),
        compiler_params=_cparams(1),
        name="modulation",
    )(cvec, w_mod, b_mod)


PRENORM_ROWS = 512


NORM_ROWS = 16


def _norm_mod(t, g, sh, sc):
    ms = jnp.mean(t * t, axis=-1, keepdims=True)
    return ((t * lax.rsqrt(ms + EPS)) * g) * (1.0 + sc) + sh


def _norm_mod_rows(src_ref, dst_ref, g_ref, sh_ref, sc_ref):
    g = g_ref[...]
    sh = sh_ref[0]
    sc = sc_ref[0]

    def body(k, carry):
        rows = pl.ds(pl.multiple_of(k * NORM_ROWS, NORM_ROWS), NORM_ROWS)
        dst_ref[rows, :] = _norm_mod(src_ref[rows, :], g, sh, sc).astype(dst_ref.dtype)
        return carry

    lax.fori_loop(0, src_ref.shape[0] // NORM_ROWS, body, 0, unroll=8)


def _prenorm1_kernel(x_ref, ctx_ref, g_ref, sh_ref, sc_ref, wdt_ref, bdt_ref, o_ref, dt_ref, *, n_lat_tiles):
    i = pl.program_id(0)

    @pl.when(i < n_lat_tiles)
    def _():
        _norm_mod_rows(x_ref, o_ref, g_ref, sh_ref, sc_ref)

    @pl.when(i >= n_lat_tiles)
    def _():
        _norm_mod_rows(ctx_ref, o_ref, g_ref, sh_ref, sc_ref)

    raw = lax.dot_general(wdt_ref[...], o_ref[...], (((1,), (1,)), ((), ())),
                          preferred_element_type=F32) + bdt_ref[...]
    dt = jnp.maximum(raw, 0.0) + jnp.log1p(jnp.exp(-jnp.abs(raw)))
    for c in range(dt_ref.shape[0]):
        dt_ref[c] = dt[:, c * CHUNK:(c + 1) * CHUNK]


def _prenorm1(x2d, ctx2d, norm_g, mod3, batch, shift_blk, scale_blk, w_dt_t, dt_bias_col):
    n_lat, d = x2d.shape
    n_ctx = ctx2d.shape[0]
    n_dt = w_dt_t.shape[0]
    tr = math.gcd(PRENORM_ROWS, math.gcd(n_lat // batch, n_ctx))
    lat_tiles, ctx_tiles = n_lat // tr, n_ctx // tr
    tiles_per_batch = lat_tiles // batch

    def mod_row(i):
        return jnp.where(i < lat_tiles, i // tiles_per_batch, batch)

    return pl.pallas_call(
        functools.partial(_prenorm1_kernel, n_lat_tiles=lat_tiles),
        grid=(lat_tiles + ctx_tiles,),
        in_specs=[pl.BlockSpec((tr, d), lambda i: (jnp.minimum(i, lat_tiles - 1), 0)),
                  pl.BlockSpec((tr, d), lambda i: (jnp.maximum(i - lat_tiles, 0), 0)),
                  pl.BlockSpec((1, d), lambda i: (0, 0)),
                  pl.BlockSpec((1, 1, d), lambda i: (mod_row(i), 0, shift_blk)),
                  pl.BlockSpec((1, 1, d), lambda i: (mod_row(i), 0, scale_blk)),
                  pl.BlockSpec((n_dt, d), lambda i: (0, 0)),
                  pl.BlockSpec((n_dt, 1), lambda i: (0, 0))],
        out_specs=[pl.BlockSpec((tr, d), lambda i: (i, 0)),
                   pl.BlockSpec((tr // CHUNK, n_dt, CHUNK), lambda i: (i, 0, 0))],
        out_shape=[jax.ShapeDtypeStruct((n_lat + n_ctx, d), BF16),
                   jax.ShapeDtypeStruct(((n_lat + n_ctx) // CHUNK, n_dt, CHUNK), F32)],
        compiler_params=_cparams(1),
        name="prenorm1",
    )(x2d, ctx2d, norm_g, mod3, mod3, w_dt_t, dt_bias_col)


OUT_PROJ_ROWS = 512
OUT_PROJ_SECTIONS = 2


def _out_proj_kernel(a_ref, w_ref, x_ref, gt_ref, g_ref, sh_ref, sc_ref, xmid_ref, h_ref, w_scr):
    @pl.when(pl.program_id(0) == 0)
    def _():
        w_scr[...] = w_ref[...].astype(BF16)

    half = a_ref.shape[0] // OUT_PROJ_SECTIONS
    g, sh, sc, gt = g_ref[...], sh_ref[0], sc_ref[0], gt_ref[0]
    for h0 in range(0, a_ref.shape[0], half):
        xmid_ref[h0:h0 + half, :] = x_ref[h0:h0 + half, :] + gt * jnp.dot(
            a_ref[h0:h0 + half, :], w_scr[...], preferred_element_type=F32)
        for r in range(h0, h0 + half, NORM_ROWS):
            h_ref[r:r + NORM_ROWS, :] = _norm_mod(xmid_ref[r:r + NORM_ROWS, :], g, sh, sc).astype(h_ref.dtype)


def _out_proj(a, w, x2d, norm_g, mod3, batch, gate_blk, shift_blk, scale_blk):
    n_lat, d = x2d.shape
    tr = math.gcd(OUT_PROJ_ROWS, n_lat // batch)
    tiles_per_batch = n_lat // tr // batch

    def mod_spec(blk):
        return pl.BlockSpec((1, 1, d), lambda i: (i // tiles_per_batch, 0, blk))

    return pl.pallas_call(
        _out_proj_kernel,
        grid=(n_lat // tr,),
        in_specs=[pl.BlockSpec((tr, a.shape[1]), lambda i: (i, 0)),
                  pl.BlockSpec(w.shape, lambda i: (0, 0), pipeline_mode=pl.Buffered(1)),
                  pl.BlockSpec((tr, d), lambda i: (i, 0)),
                  mod_spec(gate_blk),
                  pl.BlockSpec((1, d), lambda i: (0, 0)),
                  mod_spec(shift_blk),
                  mod_spec(scale_blk)],
        out_specs=[pl.BlockSpec((tr, d), lambda i: (i, 0)),
                   pl.BlockSpec((tr, d), lambda i: (i, 0))],
        out_shape=[jax.ShapeDtypeStruct((n_lat, d), F32),
                   jax.ShapeDtypeStruct((n_lat, d), BF16)],
        scratch_shapes=[pltpu.VMEM(w.shape, BF16)],
        compiler_params=_cparams(1),
        name="out_proj",
    )(a, w, x2d, mod3, norm_g, mod3, mod3)


def _mm_kernel(*refs, n_a, which_a, n_extra, epilogue, w_t):
    n_w = len(which_a)
    a_refs = refs[:n_a]
    w_refs = refs[n_a:n_a + n_w]
    e_refs = refs[n_a + n_w:n_a + n_w + n_extra]
    o_ref = refs[n_a + n_w + n_extra]
    w_scr = refs[n_a + n_w + n_extra + 1:]

    @pl.when(pl.program_id(1) == 0)
    def _():
        for k in range(n_w):
            w_scr[k][...] = w_refs[k][...].astype(BF16)

    dims = (((1,), (1,)), ((), ())) if w_t else (((1,), (0,)), ((), ()))
    accs = [lax.dot_general(a_refs[which_a[k]][...], w_scr[k][...], dims, preferred_element_type=F32)
            for k in range(n_w)]
    o_ref[...] = epilogue(accs, e_refs).astype(o_ref.dtype)


def _matmul(a_list, w_list, extras, epilogue, n_out, out_dtype, *, m_rows, tm, tn, name, w_t=False):
    grid = (n_out // tn, m_rows // tm)
    in_specs, args, scratch = [], [], []
    for a, off in a_list:
        in_specs.append(pl.BlockSpec((tm, a.shape[1]), lambda j, m, off=off: (m + off, 0)))
        args.append(a)
    for w, off, _ in w_list:
        if w_t:
            in_specs.append(pl.BlockSpec((pl.Element(tn), pl.Element(w.shape[1])),
                                         lambda j, m, off=off: (pl.multiple_of(off(j), 8), 0)))
            scratch.append(pltpu.VMEM((tn, w.shape[1]), BF16))
        else:
            in_specs.append(pl.BlockSpec((w.shape[0], tn), lambda j, m, off=off: (0, j + off)))
            scratch.append(pltpu.VMEM((w.shape[0], tn), BF16))
        args.append(w)
    for e, blk, imap in extras:
        in_specs.append(pl.BlockSpec(blk, imap))
        args.append(e)
    kern = functools.partial(_mm_kernel, n_a=len(a_list), which_a=tuple(w[2] for w in w_list),
                             n_extra=len(extras), epilogue=epilogue, w_t=w_t)
    return pl.pallas_call(
        kern,
        grid=grid,
        in_specs=in_specs,
        out_specs=pl.BlockSpec((tm, tn), lambda j, m: (m, j)),
        out_shape=jax.ShapeDtypeStruct((m_rows, n_out), out_dtype),
        scratch_shapes=scratch,
        compiler_params=_cparams(2),
        name=name,
    )(*args)


def _ep_plain(accs, e_refs):
    return accs[0]


def _ep_swiglu(accs, e_refs):
    return _silu(accs[0]) * accs[1]


def _ep_merge(accs, e_refs):
    g_ssm = e_refs[0][...].astype(F32)
    g_attn = e_refs[1][...].astype(F32)
    return jax.nn.sigmoid(g_ssm) * accs[0] + jax.nn.sigmoid(g_attn) * accs[1]


def _ep_gated_residual(accs, e_refs):
    return e_refs[0][...] + e_refs[1][0] * accs[0]


def _norm_rope(t, g, cs, sn, scale):
    tn = (t * lax.rsqrt(jnp.mean(t * t, axis=-1, keepdims=True) + EPS)) * g
    lane = lax.broadcasted_iota(jnp.int32, tn.shape, 1)
    first_half = (lane % (HEAD_DIM // 2)) < (HEAD_DIM // 4)
    rot = jnp.where(first_half,
                    pltpu.roll(tn, HEAD_DIM - HEAD_DIM // 4, 1),
                    pltpu.roll(tn, HEAD_DIM // 4, 1))
    return (tn * cs + rot * sn) * scale


def _attn_kernel(q_ref, kl_ref, kc_ref, vl_ref, vc_ref, qg_ref, kg_ref, cos_ref, sin_ref, o_ref,
                 k_scr, v_scr, *bufs, n_ctx, tq, tk, q_scale):
    seq = q_ref.shape[0]
    n_tiles = seq // tq
    kg = kg_ref[...]
    qg = qg_ref[...]

    kc = kc_ref[...].astype(F32)
    k_scr[0:n_ctx, :] = ((kc * lax.rsqrt(jnp.mean(kc * kc, axis=-1, keepdims=True) + EPS)) * kg).astype(BF16)

    def k_body(i, carry):
        r0 = pl.multiple_of(i * tk, tk)
        kt = _norm_rope(kl_ref[pl.ds(r0, tk), :].astype(F32), kg,
                        cos_ref[pl.ds(r0, tk), :], sin_ref[pl.ds(r0, tk), :], 1.0)
        k_scr[pl.ds(pl.multiple_of(n_ctx + r0, tk), tk), :] = kt.astype(BF16)
        return carry

    lax.fori_loop(0, seq // tk, k_body, 0, unroll=2)
    v_scr[0:n_ctx, 0:HEAD_DIM] = vc_ref[...]
    v_scr[n_ctx:, 0:HEAD_DIM] = vl_ref[...]
    v_scr[:, HEAD_DIM:] = jnp.ones((v_scr.shape[0], HEAD_DIM), BF16)

    s_bufs = bufs[:Q_REP]
    p_bufs = bufs[Q_REP:]

    def scores(t, r):
        r0 = pl.multiple_of(t * tq, tq)
        q = _norm_rope(q_ref[pl.ds(r0, tq), r * HEAD_DIM:(r + 1) * HEAD_DIM].astype(F32), qg,
                       cos_ref[pl.ds(r0, tq), :], sin_ref[pl.ds(r0, tq), :], q_scale).astype(BF16)
        s_bufs[r][...] = lax.dot_general(q, k_scr[...], (((1,), (1,)), ((), ())),
                                             preferred_element_type=F32)

    def softmax(r):
        sc = s_bufs[r][...]
        mx = jnp.max(sc, axis=-1, keepdims=True)
        p_bufs[r][...] = jnp.exp2((sc - mx).astype(BF16))

    def values(t, r):
        r0 = pl.multiple_of(t * tq, tq)
        o = jnp.dot(p_bufs[r][...], v_scr[...], preferred_element_type=F32)
        den = o[:, HEAD_DIM:HEAD_DIM + 1]
        o_ref[pl.ds(r0, tq), r * HEAD_DIM:(r + 1) * HEAD_DIM] = (o[:, :HEAD_DIM] / den).astype(o_ref.dtype)

    def tile(t, first):
        for r in range(Q_REP):
            if r >= 1:
                softmax(r - 1)
            elif not first:
                softmax(Q_REP - 1)
            if r >= 2:
                values(t, r - 2)
            elif not first:
                values(t - 1, Q_REP - 2 + r)
            scores(t, r)

    tile(0, True)
    n_peel = 1 + (n_tiles - 1) % 4
    for t in range(1, n_peel):
        tile(t, False)

    def body(t, carry):
        tile(t, False)
        return carry

    lax.fori_loop(n_peel, n_tiles, body, 0, unroll=4)
    softmax(Q_REP - 1)
    values(n_tiles - 1, Q_REP - 2)
    values(n_tiles - 1, Q_REP - 1)


def _attention(p_lat, c_q, p_all, c_k, c_v, q_norm, k_norm, cos_t, sin_t, batch, seq, n_ctx):
    tq = 128
    tk = 256
    assert Q_REP >= 2
    ctx_blk0 = batch * seq // n_ctx
    gw = Q_REP * HEAD_DIM
    n_keys = n_ctx + seq
    qb, kb, vb = c_q // gw, c_k // HEAD_DIM, c_v // HEAD_DIM
    const = dict(pipeline_mode=pl.Buffered(1))
    return pl.pallas_call(
        functools.partial(_attn_kernel, n_ctx=n_ctx, tq=tq, tk=tk, q_scale=HEAD_DIM ** -0.5 * LOG2_E),
        grid=(batch, KV_HEADS),
        in_specs=[pl.BlockSpec((seq, gw), lambda b, g: (b, qb + g)),
                  pl.BlockSpec((seq, HEAD_DIM), lambda b, g: (b, kb + g)),
                  pl.BlockSpec((n_ctx, HEAD_DIM), lambda b, g: (ctx_blk0 + b, kb + g)),
                  pl.BlockSpec((seq, HEAD_DIM), lambda b, g: (b, vb + g)),
                  pl.BlockSpec((n_ctx, HEAD_DIM), lambda b, g: (ctx_blk0 + b, vb + g)),
                  pl.BlockSpec((1, HEAD_DIM), lambda b, g: (0, 0)),
                  pl.BlockSpec((1, HEAD_DIM), lambda b, g: (0, 0)),
                  pl.BlockSpec((seq, HEAD_DIM), lambda b, g: (0, 0), **const),
                  pl.BlockSpec((seq, HEAD_DIM), lambda b, g: (0, 0), **const)],
        out_specs=pl.BlockSpec((seq, gw), lambda b, g: (b, g)),
        out_shape=jax.ShapeDtypeStruct((batch * seq, ATTN_HEADS * HEAD_DIM), BF16),
        scratch_shapes=[pltpu.VMEM((n_keys, HEAD_DIM), BF16),
                        pltpu.VMEM((n_keys, 2 * HEAD_DIM), BF16),
                        *[pltpu.VMEM((tq, n_keys), F32) for _ in range(Q_REP)],
                        *[pltpu.VMEM((tq, n_keys), BF16) for _ in range(Q_REP)]],
        compiler_params=_cparams(2),
        name="attention",
    )(p_lat, p_all, p_all, p_all, p_all, q_norm, k_norm, cos_t, sin_t)


def _split3(v):
    hi = v.astype(BF16).astype(F32)
    r1 = v - hi
    mid = r1.astype(BF16).astype(F32)
    lo = (r1 - mid).astype(BF16).astype(F32)
    return hi, mid, lo


CONV_HALO = 16


def _conv_shift_matrix():
    pad = (CONV_W - 1) // 2
    taps = [j for j in range(CONV_W) if j != pad]
    rows = lax.broadcasted_iota(jnp.int32, (len(taps) * CHUNK, CHUNK + 2 * CONV_HALO), 0)
    cols = lax.broadcasted_iota(jnp.int32, (len(taps) * CHUNK, CHUNK + 2 * CONV_HALO), 1)
    tap = jnp.where(rows // CHUNK < pad, rows // CHUNK, rows // CHUNK + 1)
    return (cols == CONV_HALO - pad + tap + rows % CHUNK).astype(BF16), taps


def _conv_block_fn(streams, n_rows, shift, taps):
    n_blk = n_rows // CHUNK
    pad = (CONV_W - 1) // 2

    def lanes(parts):
        return parts[0] if len(parts) == 1 else jnp.concatenate(parts, axis=1)

    def block(i):
        r0 = pl.multiple_of(i * CHUNK, CHUNK)
        lo = pl.multiple_of(jnp.maximum(r0 - CONV_HALO, 0), CONV_HALO)
        hi = pl.multiple_of(jnp.minimum(r0 + CHUNK, n_rows - CONV_HALO), CONV_HALO)
        out = []
        for src_refs, w_refs, b_refs, store in streams:
            cur = lanes([s[pl.ds(r0, CHUNK), :] for s in src_refs])
            prev = lanes([s[pl.ds(lo, CONV_HALO), :] for s in src_refs])
            nxt = lanes([s[pl.ds(hi, CONV_HALO), :] for s in src_refs])
            prev = jnp.where(i > 0, prev, jnp.zeros_like(prev))
            nxt = jnp.where(i < n_blk - 1, nxt, jnp.zeros_like(nxt))
            window = jnp.concatenate([prev, cur, nxt], axis=0)
            shifted = jnp.dot(shift, window, preferred_element_type=F32)
            w = lanes([r[...] for r in w_refs])
            acc = lanes([r[...] for r in b_refs]) + w[pad:pad + 1, :] * cur.astype(F32)
            for k, j in enumerate(taps):
                acc = acc + w[j:j + 1, :] * shifted[k * CHUNK:(k + 1) * CHUNK, :]
            out.append(store(r0, _silu(acc)))
        return out

    return block


def _ssd_kernel(x_lat, b_lat, c_lat, x_ctx, b_ctx, c_ctx,
                cwx, cwb, cwc, cbx, cbb, cbc,
                dt_lat, dt_ctx, aneg_ref, dskip_ref, z_ref, ng_ref,
                o_ref,
                xc, bt, cc, y_scr, dt_scr, u_scr, xt_scr, st_f, st_b, *, seq, n_ctx):
    nh = HEADS_PER_GROUP
    n_ctx_chunks = n_ctx // CHUNK
    n_lat_chunks = seq // CHUNK
    n_chunks = n_ctx_chunks + n_lat_chunks

    def store_x(off):
        def f(r0, v):
            xc[pl.ds(pl.multiple_of(off + r0, CHUNK), CHUNK), :] = v
            return v
        return f

    def store_bc(off):
        def f(r0, v):
            r = pl.multiple_of(off + r0, CHUNK)
            b_t = v[:, :D_STATE].T.astype(BF16)
            c_m = v[:, D_STATE:].astype(BF16)
            bt[:, pl.ds(r, CHUNK)] = b_t
            cc[pl.ds(r, CHUNK), :] = c_m
            return b_t, c_m
        return f

    shift, taps = _conv_shift_matrix()
    conv_ctx = _conv_block_fn([([x_ctx], [cwx], [cbx], store_x(0)),
                               ([b_ctx, c_ctx], [cwb, cwc], [cbb, cbc], store_bc(0))], n_ctx, shift, taps)
    conv_lat = _conv_block_fn([([x_lat], [cwx], [cbx], store_x(n_ctx)),
                               ([b_lat, c_lat], [cwb, cwc], [cbb, cbc], store_bc(n_ctx))], seq, shift, taps)
    for c in range(n_ctx_chunks):
        conv_ctx(c)

    row = lax.broadcasted_iota(jnp.int32, (CHUNK, CHUNK), 0)
    col = lax.broadcasted_iota(jnp.int32, (CHUNK, CHUNK), 1)
    triu = (row <= col).astype(BF16)
    eye = (row == col).astype(BF16)
    lower = row >= col
    below = row > col
    above = row < col
    dskip = dskip_ref[...]
    lane_head = lax.broadcasted_iota(jnp.int32, (CHUNK, GROUP_CH), 1) // SSM_HEAD_DIM

    def spread(n_cols, q_of_col, hd_of_col):
        k = lax.broadcasted_iota(jnp.int32, (CHUNK, n_cols), 0)
        c = lax.broadcasted_iota(jnp.int32, (CHUNK, n_cols), 1)
        return ((k < 72) & (k // 24 == q_of_col(c)) & (k % 8 == hd_of_col(c))).astype(BF16)

    n_cb = 2 * nh * CHUNK
    rmat_f = spread(n_cb + 2 * GROUP_CH,
                    lambda c: jnp.where(c < n_cb, 0, jnp.where(c < n_cb + GROUP_CH, 1, 2)),
                    lambda c: jnp.where(c < n_cb, c // CHUNK, ((c - n_cb) % GROUP_CH) // SSM_HEAD_DIM))
    rmat_b = spread(2 * GROUP_CH,
                    lambda c: jnp.where(c < GROUP_CH, 1, 2),
                    lambda c: nh + (c % GROUP_CH) // SSM_HEAD_DIM)

    dt_scr[0:n_ctx_chunks] = dt_ctx[...]
    dt_scr[n_ctx_chunks:n_chunks] = dt_lat[...]
    dt3 = dt_scr[...]
    n_rows = n_chunks * 8
    dt2 = dt3.reshape(n_rows, CHUNK)
    a2 = (dt3 * aneg_ref[...][None]).reshape(n_rows, CHUNK)
    cum = sum(jnp.dot(part.astype(BF16), triu, preferred_element_type=F32) for part in _split3(a2))
    total = jnp.broadcast_to(cum[:, CHUNK - 1:CHUNK], cum.shape)
    cumx = cum - a2
    is_fwd = (lax.broadcasted_iota(jnp.int32, (n_rows, CHUNK), 0) % 8) < nh
    u2 = jnp.where(is_fwd, cum, cumx) * LOG2_E
    w2 = dt2 * jnp.exp(jnp.where(is_fwd, total - cum, cumx))
    e2 = jnp.exp(jnp.where(is_fwd, cum, total - cumx))
    u_scr[...] = u2.reshape(n_chunks, 8, CHUNK)
    pieces = [p.reshape(n_chunks, 8, CHUNK) for v in (u2, w2, e2) for p in _split3(v)]
    pieces.append(jnp.zeros((n_chunks, CHUNK - 8 * len(pieces), CHUNK), F32))
    packed = jnp.concatenate(pieces, axis=1).reshape(n_chunks * CHUNK, CHUNK).astype(BF16)
    xt_scr[...] = lax.dot_general(eye, packed, (((1,), (1,)), ((), ())),
                                  preferred_element_type=F32).astype(BF16)

    def finish(y_off_row, y):
        y = y + y_scr[pl.ds(y_off_row, CHUNK), :]
        gated = y * _silu(z_ref[pl.ds(y_off_row, CHUNK), :].astype(F32))
        ms = jnp.mean(gated * gated, axis=-1, keepdims=True)
        o_ref[pl.ds(y_off_row, CHUNK), :] = ((gated * lax.rsqrt(ms + EPS)) * ng_ref[...]).astype(o_ref.dtype)

    def emit(y_off_row, y, last):
        if last:
            finish(y_off_row, y)
        else:
            y_scr[pl.ds(y_off_row, CHUNK), :] = y

    def fwd_chunk(ci, y_off_row, st_prev, conv_vals=None, last=False):
        r0 = pl.multiple_of(ci * CHUNK, CHUNK)
        xt = xt_scr[:, pl.ds(r0, CHUNK)]
        if conv_vals is None:
            xs = xc[pl.ds(r0, CHUNK), :]
            btc = bt[:, pl.ds(r0, CHUNK)]
        else:
            xs, (btc, cm) = conv_vals
        if y_off_row is None:
            big = jnp.dot(xt, rmat_f[:, n_cb:], preferred_element_type=F32)
            w_col, e_col = big[:, :GROUP_CH], big[:, GROUP_CH:]
        else:
            big = jnp.dot(xt, rmat_f, preferred_element_type=F32)
            cb, w_col, e_col = big[:, :n_cb], big[:, n_cb:n_cb + GROUP_CH], big[:, n_cb + GROUP_CH:]
            u8 = u_scr[ci]
            dt8 = dt_scr[ci]
            if conv_vals is None:
                cm = cc[pl.ds(r0, CHUNK), :]
            g = jnp.dot(cm, btc, preferred_element_type=F32)
            xs_b = xs.astype(BF16)
            w_parts, x_parts = [], []
            for r in range(nh):
                ef = cb[:, r * CHUNK:(r + 1) * CHUNK] - u8[r:r + 1, :]
                eb = u8[nh + r:nh + r + 1, :] - cb[:, (nh + r) * CHUNK:(nh + r + 1) * CHUNK]
                dt_f = dt8[r:r + 1, :]
                dt_b = dt8[nh + r:nh + r + 1, :]
                dt_sel = jnp.where(below, dt_f, jnp.where(above, dt_b, dt_f + dt_b))
                w_parts.append((g * (jnp.exp2(jnp.where(lower, ef, eb)) * dt_sel)).astype(BF16))
                x_parts.append(jnp.where(lane_head == r, xs_b, jnp.zeros_like(xs_b)))
            y = jnp.dot(jnp.concatenate(w_parts, axis=1), jnp.concatenate(x_parts, axis=0),
                        preferred_element_type=F32)
            y = y + e_col * jnp.dot(cm, st_prev.astype(BF16), preferred_element_type=F32)
            emit(y_off_row, y + dskip * xs, last)
        xw = (xs * w_col).astype(BF16)
        return st_prev * e_col[CHUNK - 1:CHUNK, :] + jnp.dot(btc, xw, preferred_element_type=F32)

    def bwd_chunk(ci, y_off_row, st_prev, conv_vals=None, last=False):
        r0 = pl.multiple_of(ci * CHUNK, CHUNK)
        xt = xt_scr[:, pl.ds(r0, CHUNK)]
        if conv_vals is None:
            xs = xc[pl.ds(r0, CHUNK), :]
            btc = bt[:, pl.ds(r0, CHUNK)]
        else:
            xs, (btc, cm) = conv_vals
        big = jnp.dot(xt, rmat_b, preferred_element_type=F32)
        w_col, e_col = big[:, :GROUP_CH], big[:, GROUP_CH:]
        if y_off_row is not None:
            if conv_vals is None:
                cm = cc[pl.ds(r0, CHUNK), :]
            emit(y_off_row, e_col * jnp.dot(cm, st_prev.astype(BF16), preferred_element_type=F32), last)
        xw = (xs * w_col).astype(BF16)
        return st_prev * e_col[0:1, :] + jnp.dot(btc, xw, preferred_element_type=F32)

    st = jnp.zeros((D_STATE, GROUP_CH), F32)
    for c in range(n_ctx_chunks):
        st = fwd_chunk(c, None, st)
    st_f[...] = st
    st = jnp.zeros((D_STATE, GROUP_CH), F32)
    for c in reversed(range(n_ctx_chunks)):
        st = bwd_chunk(c, None, st)
    st_b[...] = st

    assert n_lat_chunks % 2 == 0
    half = n_lat_chunks // 2
    unroll = math.gcd(SSD_UNROLL, half)

    def row_of(c):
        return pl.multiple_of(c * CHUNK, CHUNK)

    def first_half(i, carry):
        sf, sb = st_f[...], st_b[...]
        vf = conv_lat(i * unroll)
        vb = conv_lat(n_lat_chunks - 1 - i * unroll)
        for u in range(unroll):
            kf = i * unroll + u
            kb = n_lat_chunks - 1 - kf
            nf = conv_lat(kf + 1) if u + 1 < unroll else None
            nb = conv_lat(kb - 1) if u + 1 < unroll else None
            sf = fwd_chunk(n_ctx_chunks + kf, row_of(kf), sf, vf)
            sb = bwd_chunk(n_ctx_chunks + kb, row_of(kb), sb, vb)
            vf, vb = nf, nb
        st_f[...] = sf
        st_b[...] = sb
        return carry

    lax.fori_loop(0, half // unroll, first_half, 0)

    def second_half(i, carry):
        sf, sb = st_f[...], st_b[...]
        for u in range(unroll):
            kf = half + i * unroll + u
            kb = n_lat_chunks - 1 - kf
            sf = fwd_chunk(n_ctx_chunks + kf, row_of(kf), sf, last=True)
            sb = bwd_chunk(n_ctx_chunks + kb, row_of(kb), sb, last=True)
        st_f[...] = sf
        st_b[...] = sb
        return carry

    lax.fori_loop(0, half // unroll, second_half, 0)


def _ssd(xbc_all, conv_w, conv_b, dt_c, aneg_col, dskip_row, z, norm_g, batch, seq, n_ctx):
    d_ssm = SSM_HEADS * SSM_HEAD_DIM
    b0 = d_ssm // D_STATE
    c0 = b0 + SSM_GROUPS
    ctx_blk0 = batch * seq // n_ctx
    n_ctx_chunks, n_lat_chunks = n_ctx // CHUNK, seq // CHUNK
    n_tok = n_ctx + seq
    in_specs = [
        pl.BlockSpec((seq, GROUP_CH), lambda b, g: (b, g)),
        pl.BlockSpec((seq, D_STATE), lambda b, g: (b, b0 + g)),
        pl.BlockSpec((seq, D_STATE), lambda b, g: (b, c0 + g)),
        pl.BlockSpec((n_ctx, GROUP_CH), lambda b, g: (ctx_blk0 + b, g)),
        pl.BlockSpec((n_ctx, D_STATE), lambda b, g: (ctx_blk0 + b, b0 + g)),
        pl.BlockSpec((n_ctx, D_STATE), lambda b, g: (ctx_blk0 + b, c0 + g)),
        pl.BlockSpec((CONV_W, GROUP_CH), lambda b, g: (0, g)),
        pl.BlockSpec((CONV_W, D_STATE), lambda b, g: (0, b0 + g)),
        pl.BlockSpec((CONV_W, D_STATE), lambda b, g: (0, c0 + g)),
        pl.BlockSpec((1, GROUP_CH), lambda b, g: (0, g)),
        pl.BlockSpec((1, D_STATE), lambda b, g: (0, b0 + g)),
        pl.BlockSpec((1, D_STATE), lambda b, g: (0, c0 + g)),
        pl.BlockSpec((n_lat_chunks, 8, CHUNK), lambda b, g: (b, g, 0)),
        pl.BlockSpec((n_ctx_chunks, 8, CHUNK), lambda b, g: (ctx_blk0 + b, g, 0)),
        pl.BlockSpec((8, 1), lambda b, g: (g, 0)),
        pl.BlockSpec((1, GROUP_CH), lambda b, g: (0, g)),
        pl.BlockSpec((seq, GROUP_CH), lambda b, g: (b, g)),
        pl.BlockSpec((1, GROUP_CH), lambda b, g: (0, g)),
    ]
    return pl.pallas_call(
        functools.partial(_ssd_kernel, seq=seq, n_ctx=n_ctx),
        grid=(batch, SSM_GROUPS),
        in_specs=in_specs,
        out_specs=pl.BlockSpec((seq, GROUP_CH), lambda b, g: (b, g)),
        out_shape=jax.ShapeDtypeStruct((batch * seq, d_ssm), BF16),
        scratch_shapes=[pltpu.VMEM((n_tok, GROUP_CH), F32),
                        pltpu.VMEM((D_STATE, n_tok), BF16),
                        pltpu.VMEM((n_tok, D_STATE), BF16),
                        pltpu.VMEM((seq, GROUP_CH), F32),
                        pltpu.VMEM((n_tok // CHUNK, 8, CHUNK), F32),
                        pltpu.VMEM((n_tok // CHUNK, 8, CHUNK), F32),
                        pltpu.VMEM((CHUNK, n_tok), BF16),
                        pltpu.VMEM((D_STATE, GROUP_CH), F32),
                        pltpu.VMEM((D_STATE, GROUP_CH), F32)],
        compiler_params=_cparams(2),
        name="ssd",
    )(xbc_all, xbc_all, xbc_all, xbc_all, xbc_all, xbc_all,
      conv_w, conv_w, conv_w, conv_b, conv_b, conv_b,
      dt_c, dt_c, aneg_col, dskip_row, z, norm_g)


def _rope_tables(seq):
    n_freq = HEAD_DIM // 4
    pos = np.arange(seq)
    inv = np.power(ROPE_THETA, -np.arange(n_freq, dtype=np.float64) / n_freq)
    ang_r = (pos // GRID_W)[:, None] * inv
    ang_c = (pos % GRID_W)[:, None] * inv
    cos_t = np.concatenate([np.cos(ang_r), np.cos(ang_r), np.cos(ang_c), np.cos(ang_c)], axis=1)
    sin_t = np.concatenate([-np.sin(ang_r), np.sin(ang_r), -np.sin(ang_c), np.sin(ang_c)], axis=1)
    return jnp.asarray(cos_t, F32), jnp.asarray(sin_t, F32)


def kernel(x, c, ctx, c_ctx, w_mod, b_mod, norm1, w_in, conv_w, conv_b, dt_bias, a_log, d_skip,
           ssm_norm, q_norm, k_norm, w_ssm_br, w_attn_br, w_o, norm2, w_ffn_in, w_ffn_out):
    batch, seq, d = x.shape
    n_ctx = ctx.shape[1]
    n_lat = batch * seq
    n_ctx_rows = batch * n_ctx
    n_all = n_lat + n_ctx_rows
    d_ssm = SSM_HEADS * SSM_HEAD_DIM
    d_conv = d_ssm + 2 * SSM_GROUPS * D_STATE
    d_attn = ATTN_HEADS * HEAD_DIM
    d_kv = KV_HEADS * HEAD_DIM
    d_ff = w_ffn_out.shape[1]
    n_dt = 2 * SSM_HEADS
    o_xbc = d_ssm
    o_dt = o_xbc + d_conv
    o_q = o_dt + n_dt
    o_k = o_q + d_attn
    o_g = o_k + 2 * d_kv
    assert w_mod.shape[0] == 1, "single-layer block"

    cvec = jnp.concatenate([c, c_ctx[None, :], jnp.zeros((8 - batch - 1, d), F32)], axis=0)
    mod = _modulation(cvec, w_mod[0], b_mod[0][None, :])
    mod3 = mod[:batch + 1].reshape(batch + 1, 1, N_MOD * d)

    w_in_t = jnp.transpose(w_in[0])
    perm = np.array([dr * SSM_HEADS + g * HEADS_PER_GROUP + r
                     for g in range(SSM_GROUPS) for dr in range(2) for r in range(HEADS_PER_GROUP)])
    w_dt_t = w_in_t[o_dt:o_dt + n_dt][perm].astype(BF16)
    dt_bias_col = dt_bias[0].reshape(n_dt)[perm][:, None]
    aneg_col = (-jnp.exp(a_log[0].astype(F32))).reshape(n_dt)[perm][:, None]

    x2d = x.reshape(n_lat, d)
    h_all, dt_c = _prenorm1(x2d, ctx.reshape(n_ctx_rows, d), norm1, mod3, batch, 0, 1, w_dt_t, dt_bias_col)

    tn = 1024
    tm_lat = 1024 if seq % 1024 == 0 else 512
    tm_all = n_all // 8 if (n_all // 8) % 16 == 0 and n_all % 8 == 0 else 512

    def in_proj(sections, rows, tm_rows, name):
        starts = np.cumsum([0] + [w // tn for _, w in sections])

        def row_off(j):
            off = sections[0][0] + j * tn
            for (first, _), s in zip(sections[1:], starts[1:]):
                off = jnp.where(j >= s, first + (j - s) * tn, off)
            return off

        return _matmul([(h_all, 0)], [(w_in_t, row_off, 0)], [], _ep_plain, int(starts[-1]) * tn, BF16,
                       m_rows=rows, tm=tm_rows, tn=tn, name=name, w_t=True)

    p_lat = in_proj([(0, d_ssm), (o_q, d_attn), (o_g, 2 * d)], n_lat, tm_lat, "proj_latent")
    p_all = in_proj([(o_xbc, d_conv), (o_k, 2 * d_kv)], n_all, tm_all, "proj_all")
    c_q = d_ssm
    c_g = d_ssm + d_attn
    c_k = d_conv
    c_v = d_conv + d_kv

    dskip_row = jnp.repeat(d_skip[0].astype(F32), SSM_HEAD_DIM)[None, :]
    y_norm = _ssd(p_all, conv_w[0], conv_b[0][None, :], dt_c, aneg_col, dskip_row, p_lat,
                  ssm_norm, batch, seq, n_ctx)

    cos_t, sin_t = _rope_tables(seq)
    attn = _attention(p_lat, c_q, p_all, c_k, c_v, q_norm, k_norm, cos_t, sin_t, batch, seq, n_ctx)

    tn_m = 512
    merged = _matmul([(y_norm, 0), (attn, 0)], [(w_ssm_br[0], 0, 0), (w_attn_br[0], 0, 1)],
                     [(p_lat, (tm_lat, tn_m), lambda j, m: (m, c_g // tn_m + j)),
                      (p_lat, (tm_lat, tn_m), lambda j, m: (m, (c_g + d) // tn_m + j))],
                     _ep_merge, d, BF16, m_rows=n_lat, tm=tm_lat, tn=tn_m, name="branch_merge")
    x_mid, h2 = _out_proj(merged, w_o[0], x2d, norm2, mod3, batch, 2, 3, 4)

    tn_f = 512
    act = _matmul([(h2, 0)], [(w_ffn_in[0], 0, 0), (w_ffn_in[0], d_ff // tn_f, 0)], [], _ep_swiglu,
                  d_ff, BF16, m_rows=n_lat, tm=tm_lat, tn=tn_f, name="ffn_in")
    tm_o = 512
    tiles_per_batch_o = seq // tm_o
    out = _matmul([(act, 0)], [(w_ffn_out[0], 0, 0)],
                  [(x_mid, (tm_o, tn_f), lambda j, m: (m, j)),
                   (mod3, (1, 1, tn_f), lambda j, m: (m // tiles_per_batch_o, 0, 5 * (d // tn_f) + j))],
                  _ep_gated_residual, d, F32, m_rows=n_lat, tm=tm_o, tn=tn_f, name="ffn_out")
    return out.reshape(batch, seq, d)
```
